```python
import math
import jax, jax.numpy as jnp
from jax import lax
import numpy as np

D_MODEL = 1024
BATCH = 8
SEQ = 4096
DEPTH = 2

HEAD_DIM = 64
BLOCK = 128
A_HEADS = 4
A_QK = 2 * HEAD_DIM
A_V = 2 * HEAD_DIM
A_WIDTH = A_HEADS * A_V
B_HEADS = 4
B_KV = 2
WINDOW = 128
B_WIDTH = B_HEADS * HEAD_DIM
C_HEADS = 4
C_KV = 2
C_WIDTH = C_HEADS * HEAD_DIM
GRID_W = 64
ROPE_THETA = 10000.0
MIX_WIDTH = A_WIDTH + B_WIDTH + C_WIDTH
SPLIT_SIZES = (A_HEADS * A_QK, A_HEADS * A_QK, A_HEADS * A_V,
               B_HEADS * HEAD_DIM, B_KV * HEAD_DIM, B_KV * HEAD_DIM,
               C_HEADS * HEAD_DIM, C_KV * HEAD_DIM, C_KV * HEAD_DIM)
IN_WIDTH = sum(SPLIT_SIZES)
D_FF = 4 * D_MODEL
EPS = 1e-6
N_ALIBI = A_HEADS + B_HEADS
ALIBI_SLOPES = tuple(2.0 ** (-8.0 * (i + 1) / N_ALIBI) for i in range(N_ALIBI))
A_SLOPES = ALIBI_SLOPES[0::2]
B_SLOPES = ALIBI_SLOPES[1::2]

kernel_name = "hybrid_parallel_diff_window_axial_encoder"


def _split_points():
    pts, acc = [], 0
    for s in SPLIT_SIZES[:-1]:
        acc += s
        pts.append(acc)
    return pts


def rms_norm(x, g):
    xf = x.astype(jnp.float32)
    y = xf * lax.rsqrt(jnp.mean(xf * xf, axis=-1, keepdims=True) + EPS)
    return (y * g.astype(jnp.float32)).astype(x.dtype)


def diff_attention(q, k, v, lam, lam_init, subln_g):
    bsz, s_len = q.shape[0], q.shape[1]
    nb = s_len // BLOCK
    scale = HEAD_DIM ** -0.5
    slopes = jnp.asarray(A_SLOPES, dtype=jnp.float32)
    kpos = jnp.arange(s_len)
    qb = q.reshape(bsz, nb, BLOCK, A_HEADS, 2, HEAD_DIM).transpose(1, 0, 2, 3, 4, 5)

    def block(args):
        qi, i = args
        sc = jnp.einsum('bqhcd,bshcd->bchqs', qi, k).astype(jnp.float32) * scale
        qpos = i * BLOCK + jnp.arange(BLOCK)
        dist = jnp.abs(qpos[:, None] - kpos[None, :]).astype(jnp.float32)
        sc = sc - slopes[:, None, None] * dist[None]
        p = jax.nn.softmax(sc, axis=-1)
        attn = p[:, 0] - lam * p[:, 1]
        return jnp.einsum('bhqs,bshe->bqhe', attn.astype(v.dtype), v)

    o = lax.map(block, (qb, jnp.arange(nb)))
    o = o.transpose(1, 0, 2, 3, 4).reshape(bsz, s_len, A_HEADS, A_V)
    o = rms_norm(o, subln_g) * (1.0 - lam_init)
    return o.reshape(bsz, s_len, A_WIDTH)


def window_attention(q, k, v, sink):
    bsz, s_len = q.shape[0], q.shape[1]
    nb = s_len // BLOCK
    grp = B_HEADS // B_KV
    scale = HEAD_DIM ** -0.5
    qb = q.reshape(bsz, nb, BLOCK, B_KV, grp, HEAD_DIM)

    def band(t):
        tp = jnp.pad(t, ((0, 0), (BLOCK, BLOCK), (0, 0), (0, 0)))
        tb = tp.reshape(bsz, nb + 2, BLOCK, B_KV, HEAD_DIM)
        return jnp.concatenate([tb[:, :-2], tb[:, 1:-1], tb[:, 2:]], axis=2)

    kw, vw = band(k), band(v)
    sc = jnp.einsum('bnqkgd,bnskd->bnkgqs', qb, kw).astype(jnp.float32) * scale
    j = jnp.arange(3 * BLOCK)
    a = jnp.arange(BLOCK)
    delta = j[None, :] - BLOCK - a[:, None]
    spos = jnp.arange(nb)[:, None] * BLOCK - BLOCK + j[None, :]
    mask = (jnp.abs(delta) <= WINDOW)[None] & ((spos >= 0) & (spos < s_len))[:, None, :]
    slopes = jnp.asarray(B_SLOPES, dtype=jnp.float32).reshape(B_KV, grp)
    bias = -slopes[:, :, None, None] * jnp.abs(delta).astype(jnp.float32)[None, None]
    sc = jnp.where(mask[None, :, None, None], sc + bias[None, None], -jnp.inf)
    sk = jnp.broadcast_to(sink.astype(jnp.float32).reshape(B_KV, grp)[None, None, :, :, None, None],
                          sc.shape[:-1] + (1,))
    p = jax.nn.softmax(jnp.concatenate([sc, sk], axis=-1), axis=-1)[..., :-1]
    o = jnp.einsum('bnkgqs,bnskd->bnqkgd', p.astype(v.dtype), vw)
    return o.reshape(bsz, s_len, B_WIDTH)


def axial_rope_tables(s_len):
    rows = s_len // GRID_W
    row = jnp.repeat(jnp.arange(rows), GRID_W).astype(jnp.float32)
    col = jnp.tile(jnp.arange(GRID_W), rows).astype(jnp.float32)
    half = HEAD_DIM // 2
    freqs = ROPE_THETA ** (-jnp.arange(0, half, 2, dtype=jnp.float32) / half)
    ar = row[:, None] * freqs[None]
    ac = col[:, None] * freqs[None]
    return jnp.cos(ar), jnp.sin(ar), jnp.cos(ac), jnp.sin(ac)


def _rotate(x, cos, sin):
    x1, x2 = jnp.split(x, 2, axis=-1)
    c, s = cos[:, None, :], sin[:, None, :]
    return jnp.concatenate([x1 * c - x2 * s, x2 * c + x1 * s], axis=-1)


def apply_axial_rope(x, tabs):
    cr, sr, cc, scol = tabs
    xf = x.astype(jnp.float32)
    half = HEAD_DIM // 2
    y = jnp.concatenate([_rotate(xf[..., :half], cr, sr), _rotate(xf[..., half:], cc, scol)], axis=-1)
    return y.astype(x.dtype)


def grid_attention(q, k, v):
    bsz, s_len = q.shape[0], q.shape[1]
    nb = s_len // BLOCK
    grp = C_HEADS // C_KV
    scale = HEAD_DIM ** -0.5
    qb = q.reshape(bsz, nb, BLOCK, C_KV, grp, HEAD_DIM).transpose(1, 0, 2, 3, 4, 5)

    def block(qi):
        sc = jnp.einsum('bqkgd,bskd->bkgqs', qi, k).astype(jnp.float32) * scale
        p = jax.nn.softmax(sc, axis=-1)
        return jnp.einsum('bkgqs,bskd->bqkgd', p.astype(v.dtype), v)

    o = lax.map(block, qb)
    return o.transpose(1, 0, 2, 3, 4, 5).reshape(bsz, s_len, C_WIDTH)


def setup_inputs(seed: int = 0) -> dict:
    key = jax.random.key(seed)
    ks = jax.random.split(key, 20)
    f32 = jnp.float32

    def nrm(k, shape, scale):
        return jax.random.normal(k, shape, f32) * scale

    def gain(k, n):
        return 1.0 + 0.02 * jax.random.normal(k, (DEPTH, n), f32)

    return {
        "x": jax.random.normal(ks[0], (BATCH, SEQ, D_MODEL), f32),
        "w_in": nrm(ks[1], (DEPTH, D_MODEL, IN_WIDTH), D_MODEL ** -0.5),
        "w_out": nrm(ks[2], (DEPTH, MIX_WIDTH, D_MODEL), MIX_WIDTH ** -0.5),
        "g_pre_mix": gain(ks[3], D_MODEL),
        "g_post_mix": gain(ks[4], D_MODEL),
        "lam_q1": nrm(ks[5], (DEPTH, HEAD_DIM), 0.1),
        "lam_k1": nrm(ks[6], (DEPTH, HEAD_DIM), 0.1),
        "lam_q2": nrm(ks[7], (DEPTH, HEAD_DIM), 0.1),
        "lam_k2": nrm(ks[8], (DEPTH, HEAD_DIM), 0.1),
        "diff_subln_g": gain(ks[9], A_V),
        "sink_logits": nrm(ks[10], (DEPTH, B_HEADS), 0.5),
        "c_q_norm": gain(ks[11], HEAD_DIM),
        "c_k_norm": gain(ks[12], HEAD_DIM),
        "g_pre_mlp": gain(ks[13], D_MODEL),
        "g_post_mlp": gain(ks[14], D_MODEL),
        "w_mlp_in": nrm(ks[15], (DEPTH, D_MODEL, D_FF), D_MODEL ** -0.5),
        "w_mlp_out": nrm(ks[16], (DEPTH, D_FF, D_MODEL), D_FF ** -0.5),
    }


def reference(x, w_in, w_out, g_pre_mix, g_post_mix, lam_q1, lam_k1, lam_q2, lam_k2,
              diff_subln_g, sink_logits, c_q_norm, c_k_norm, g_pre_mlp, g_post_mlp,
              w_mlp_in, w_mlp_out):
    bsz, s_len = x.shape[0], x.shape[1]
    tabs = axial_rope_tables(s_len)
    pts = _split_points()
    for l in range(DEPTH):
        h = rms_norm(x, g_pre_mix[l])
        proj = h @ w_in[l]
        aq, ak, av, bq, bk, bv, cq, ck, cv = jnp.split(proj, pts, axis=-1)

        lam_init = 0.8 - 0.6 * math.exp(-0.3 * l)
        lam = (jnp.exp(jnp.sum(lam_q1[l].astype(jnp.float32) * lam_k1[l].astype(jnp.float32)))
               - jnp.exp(jnp.sum(lam_q2[l].astype(jnp.float32) * lam_k2[l].astype(jnp.float32)))
               + lam_init)
        a_out = diff_attention(aq.reshape(bsz, s_len, A_HEADS, 2, HEAD_DIM),
                               ak.reshape(bsz, s_len, A_HEADS, 2, HEAD_DIM),
                               av.reshape(bsz, s_len, A_HEADS, A_V),
                               lam, lam_init, diff_subln_g[l])

        b_out = window_attention(bq.reshape(bsz, s_len, B_HEADS, HEAD_DIM),
                                 bk.reshape(bsz, s_len, B_KV, HEAD_DIM),
                                 bv.reshape(bsz, s_len, B_KV, HEAD_DIM),
                                 sink_logits[l])

        cqh = apply_axial_rope(rms_norm(cq.reshape(bsz, s_len, C_HEADS, HEAD_DIM), c_q_norm[l]), tabs)
        ckh = apply_axial_rope(rms_norm(ck.reshape(bsz, s_len, C_KV, HEAD_DIM), c_k_norm[l]), tabs)
        c_out = grid_attention(cqh, ckh, cv.reshape(bsz, s_len, C_KV, HEAD_DIM))

        mix = jnp.concatenate([a_out, b_out, c_out], axis=-1) @ w_out[l]
        x = x + rms_norm(mix, g_post_mix[l])

        h = rms_norm(x, g_pre_mlp[l])
        y = jnp.square(jax.nn.relu(h @ w_mlp_in[l])) @ w_mlp_out[l]
        x = x + rms_norm(y, g_post_mlp[l])
    return x
```

```python
import functools
import math

import jax
import jax.numpy as jnp
from jax import lax
from jax.experimental import pallas as pl
from jax.experimental.pallas import tpu as pltpu

D_MODEL = 1024
HEAD_DIM = 64
LANES = 128
A_HEADS = 4
B_HEADS = 4
B_KV = 2
C_HEADS = 4
C_KV = 2
WINDOW = 128
GRID_W = 64
ROPE_THETA = 10000.0
D_FF = 4 * D_MODEL
EPS = 1e-6
N_ALIBI = A_HEADS + B_HEADS
ALIBI_SLOPES = tuple(2.0 ** (-8.0 * (i + 1) / N_ALIBI) for i in range(N_ALIBI))
A_SLOPES = ALIBI_SLOPES[0::2]
B_SLOPES = ALIBI_SLOPES[1::2]
SCALE = HEAD_DIM ** -0.5
NEG_BIG = -1e30

AQ0, AK0, AV0 = 0, 512, 1024
BQ0, BK0, BV0 = 1536, 1792, 1920
CQ0, CK0, CV0 = 2048, 2304, 2432
IN_WIDTH = 2560

VMEM_LIMIT = 56 * 1024 * 1024

F32 = jnp.float32
BF16 = jnp.bfloat16
NT_DIMS = (((1,), (1,)), ((), ()))


def _rms(x, g):
    ms = jnp.mean(x * x, axis=-1, keepdims=True)
    return x * lax.rsqrt(ms + EPS) * g


def _proj_kernel(x_ref, g_ref, w_ref, cos_ref, s1_ref, s2_ref, gq_ref, gk_ref, grp_ref,
                 aq_ref, ak_ref, av_ref, bq_ref, bk_ref, bv_ref, cq_ref, ck_ref, cv_ref):
    bm = x_ref.shape[0]
    h = _rms(x_ref[...], g_ref[...]).astype(BF16)
    low = lax.broadcasted_iota(jnp.int32, (bm, LANES), 1) < HEAD_DIM

    def seg(c0, width):
        full = jnp.dot(h, w_ref[:, c0:c0 + width], preferred_element_type=F32)
        return [full[:, j * LANES:(j + 1) * LANES] for j in range(width // LANES)]

    def put_slots(ref, j, blk):
        ref[:, (2 * j) * LANES:(2 * j + 1) * LANES] = jnp.where(low, blk, 0.0).astype(BF16)
        ref[:, (2 * j + 1) * LANES:(2 * j + 2) * LANES] = jnp.where(
            low, pltpu.roll(blk, HEAD_DIM, 1), 0.0).astype(BF16)

    def norm_rope(blk, gain):
        ms = jnp.dot(blk * blk, grp_ref[...], preferred_element_type=F32,
                     precision=lax.Precision.HIGHEST)
        y = blk * lax.rsqrt(ms + EPS) * gain
        return (y * cos_ref[...] + pltpu.roll(y, LANES - 16, 1) * s1_ref[...]
                + pltpu.roll(y, 16, 1) * s2_ref[...])

    for j, blk in enumerate(seg(AQ0, 512)):
        put_slots(aq_ref, j, blk * SCALE)
    for j, blk in enumerate(seg(AK0, 512)):
        put_slots(ak_ref, j, blk)
    for j, blk in enumerate(seg(AV0, 512)):
        av_ref[:, j * LANES:(j + 1) * LANES] = blk.astype(BF16)
    b_blks = seg(BQ0, 512)
    put_slots(bq_ref, 0, b_blks[0] * SCALE)
    put_slots(bq_ref, 1, b_blks[1] * SCALE)
    put_slots(bk_ref, 0, b_blks[2])
    put_slots(bv_ref, 0, b_blks[3])
    c_blks = seg(CQ0, 512)
    put_slots(cq_ref, 0, norm_rope(c_blks[0], gq_ref[...]) * SCALE)
    put_slots(cq_ref, 1, norm_rope(c_blks[1], gq_ref[...]) * SCALE)
    put_slots(ck_ref, 0, norm_rope(c_blks[2], gk_ref[...]))
    put_slots(cv_ref, 0, c_blks[3])


def _proj(x2, g, w, tabs, gq, gk, grp, *, seq, bm):
    n = x2.shape[0]
    nsb = seq // bm
    row = lambda i: (i, 0)
    fixed = lambda i: (0, 0)
    tab = lambda i: (i % nsb, 0)
    widths = (1024, 1024, 512, 512, 256, 256, 512, 256, 256)
    return pl.pallas_call(
        _proj_kernel,
        grid=(n // bm,),
        in_specs=[
            pl.BlockSpec((bm, D_MODEL), row),
            pl.BlockSpec((1, D_MODEL), fixed),
            pl.BlockSpec((D_MODEL, IN_WIDTH), fixed, pipeline_mode=pl.Buffered(1)),
            pl.BlockSpec((bm, LANES), tab),
            pl.BlockSpec((bm, LANES), tab),
            pl.BlockSpec((bm, LANES), tab),
            pl.BlockSpec((1, LANES), fixed),
            pl.BlockSpec((1, LANES), fixed),
            pl.BlockSpec((LANES, LANES), fixed),
        ],
        out_specs=[pl.BlockSpec((bm, wd), row) for wd in widths],
        out_shape=[jax.ShapeDtypeStruct((n, wd), BF16) for wd in widths],
        name="in_proj",
        compiler_params=pltpu.CompilerParams(
            dimension_semantics=("arbitrary",), vmem_limit_bytes=VMEM_LIMIT),
    )(x2, g, w, *tabs, gq, gk, grp)


def _attn_a_kernel(slope_ref, q_ref, k_ref, v_ref, lq1_ref, lk1_ref, lq2_ref, lk2_ref, g_ref,
                   o_ref, m_ref, l_ref, acc_ref, *, lam_init, bq, bk, seq):
    i = pl.program_id(2)
    slope = slope_ref[:, :1]
    q = q_ref[...]
    qs = (q[:, :LANES], q[:, LANES:])
    rowcol = (lax.broadcasted_iota(jnp.int32, (bq, bk), 1)
              - lax.broadcasted_iota(jnp.int32, (bq, bk), 0)).astype(F32)

    m_ref[...] = jnp.full(m_ref.shape, NEG_BIG, F32)
    l_ref[...] = jnp.zeros(l_ref.shape, F32)
    acc_ref[...] = jnp.zeros(acc_ref.shape, F32)

    def body(j, carry):
        k0 = pl.multiple_of(j * bk, bk)
        off = (j * bk - i * bq).astype(F32)
        bias = slope * jnp.abs(rowcol + off)
        kk = k_ref[pl.ds(k0, bk), :]
        vv = v_ref[pl.ds(k0, bk), :]
        for c in range(2):
            s = lax.dot_general(qs[c], kk[:, c * LANES:(c + 1) * LANES], NT_DIMS,
                                preferred_element_type=F32) - bias
            m_prev = m_ref[c]
            m_new = jnp.maximum(m_prev, jnp.max(s, axis=1, keepdims=True))
            p = jnp.exp(s - m_new[:, :1])
            alpha = jnp.exp(m_prev - m_new)
            l_ref[c] = alpha * l_ref[c] + jnp.sum(p, axis=1, keepdims=True)
            m_ref[c] = m_new
            acc_ref[c] = alpha * acc_ref[c] + jnp.dot(
                p.astype(BF16), vv, preferred_element_type=F32)
        return carry

    lax.fori_loop(0, seq // bk, body, 0)

    lam = (jnp.exp(jnp.sum(lq1_ref[...] * lk1_ref[...], axis=-1, keepdims=True))
           - jnp.exp(jnp.sum(lq2_ref[...] * lk2_ref[...], axis=-1, keepdims=True))
           + lam_init)
    o = acc_ref[0] / l_ref[0] - lam * (acc_ref[1] / l_ref[1])
    o_ref[...] = (_rms(o, g_ref[...]) * (1.0 - lam_init)).astype(BF16)


def _attn_a(slopes, aq, ak, av, lq1, lk1, lq2, lk2, g, *, lam_init, batch, seq, bq, bk):
    n = aq.shape[0]
    nq = seq // bq
    vec = lambda b, h, i: (0, 0)
    kern = functools.partial(_attn_a_kernel, lam_init=lam_init, bq=bq, bk=bk, seq=seq)
    return pl.pallas_call(
        kern,
        grid=(batch, A_HEADS, nq),
        in_specs=[
            pl.BlockSpec((None, 1, LANES), lambda b, h, i: (h, 0, 0)),
            pl.BlockSpec((bq, 2 * LANES), lambda b, h, i: (b * nq + i, h)),
            pl.BlockSpec((seq, 2 * LANES), lambda b, h, i: (b, h)),
            pl.BlockSpec((seq, LANES), lambda b, h, i: (b, h)),
            pl.BlockSpec((1, HEAD_DIM), vec),
            pl.BlockSpec((1, HEAD_DIM), vec),
            pl.BlockSpec((1, HEAD_DIM), vec),
            pl.BlockSpec((1, HEAD_DIM), vec),
            pl.BlockSpec((1, LANES), vec),
        ],
        out_specs=pl.BlockSpec((bq, LANES), lambda b, h, i: (b * nq + i, h)),
        out_shape=jax.ShapeDtypeStruct((n, A_HEADS * LANES), BF16),
        name="attn_diff",
        scratch_shapes=[
            pltpu.VMEM((2, bq, LANES), F32),
            pltpu.VMEM((2, bq, LANES), F32),
            pltpu.VMEM((2, bq, LANES), F32),
        ],
        compiler_params=pltpu.CompilerParams(
            dimension_semantics=("arbitrary", "arbitrary", "arbitrary"),
            vmem_limit_bytes=VMEM_LIMIT),
    )(slopes, aq, ak, av, lq1, lk1, lq2, lk2, g)


def _attn_b_kernel(slope_ref, sink_ref, q_ref, k_ref, v_ref, o_ref, *, bq, seq):
    i = pl.program_id(2)
    nb = seq // bq
    q = q_ref[...]
    q2 = jnp.concatenate([q[:, :LANES], q[:, LANES:]], axis=0)
    slope2 = jnp.concatenate([jnp.broadcast_to(slope_ref[0, :, :1], (bq, 1)),
                              jnp.broadcast_to(slope_ref[1, :, :1], (bq, 1))], axis=0)
    sink2 = jnp.concatenate([jnp.broadcast_to(sink_ref[0, :, :1], (bq, 1)),
                             jnp.broadcast_to(sink_ref[1, :, :1], (bq, 1))], axis=0)
    a = lax.broadcasted_iota(jnp.int32, (bq, bq), 0)
    b = lax.broadcasted_iota(jnp.int32, (bq, bq), 1)
    a = jnp.concatenate([a, a], axis=0)
    b = jnp.concatenate([b, b], axis=0)

    starts = (jnp.maximum(i - 1, 0) * bq, i * bq, jnp.minimum(i + 1, nb - 1) * bq)
    deltas = (b - bq - a, b - a, b + bq - a)
    valids = (i > 0, True, i < nb - 1)
    ss, vs = [], []
    for st, delta, ok in zip(starts, deltas, valids):
        st = pl.multiple_of(st, bq)
        kk = k_ref[pl.ds(st, bq), :]
        vs.append(v_ref[pl.ds(st, bq), :])
        s = lax.dot_general(q2, kk, NT_DIMS, preferred_element_type=F32)
        dist = jnp.abs(delta)
        s = s - slope2 * dist.astype(F32)
        mask = jnp.logical_and(dist <= WINDOW, ok)
        ss.append(jnp.where(mask, s, NEG_BIG))
    m = sink2
    for s in ss:
        m = jnp.maximum(m, jnp.max(s, axis=1, keepdims=True))
    l = jnp.exp(sink2 - m)
    acc = jnp.zeros((2 * bq, LANES), F32)
    for s, vv in zip(ss, vs):
        p = jnp.exp(s - m)
        l = l + jnp.sum(p, axis=1, keepdims=True)
        acc = acc + jnp.dot(p.astype(BF16), vv, preferred_element_type=F32)
    o = acc / l
    low = lax.broadcasted_iota(jnp.int32, (bq, LANES), 1) < HEAD_DIM
    o_ref[...] = jnp.where(low, o[:bq], pltpu.roll(o[bq:], HEAD_DIM, 1)).astype(BF16)


def _attn_b(slopes, sinks, q, k, v, *, batch, seq, bq):
    n = q.shape[0]
    nq = seq // bq
    return pl.pallas_call(
        functools.partial(_attn_b_kernel, bq=bq, seq=seq),
        grid=(batch, B_KV, nq),
        in_specs=[
            pl.BlockSpec((2, 1, LANES), lambda b, g, i: (g, 0, 0)),
            pl.BlockSpec((2, 1, LANES), lambda b, g, i: (g, 0, 0)),
            pl.BlockSpec((bq, 2 * LANES), lambda b, g, i: (b * nq + i, g)),
            pl.BlockSpec((seq, LANES), lambda b, g, i: (b, g)),
            pl.BlockSpec((seq, LANES), lambda b, g, i: (b, g)),
        ],
        out_specs=pl.BlockSpec((bq, LANES), lambda b, g, i: (b * nq + i, g)),
        out_shape=jax.ShapeDtypeStruct((n, B_HEADS * HEAD_DIM), BF16),
        name="attn_window",
        compiler_params=pltpu.CompilerParams(
            dimension_semantics=("arbitrary", "arbitrary", "arbitrary"),
            vmem_limit_bytes=VMEM_LIMIT),
    )(slopes, sinks, q, k, v)


def _attn_c_kernel(q_ref, k_ref, v_ref, o_ref, m_ref, l_ref, acc_ref, *, bq, bk, seq):
    q = q_ref[...]
    q2 = jnp.concatenate([q[:, :LANES], q[:, LANES:]], axis=0)
    m_ref[...] = jnp.full(m_ref.shape, NEG_BIG, F32)
    l_ref[...] = jnp.zeros(l_ref.shape, F32)
    acc_ref[...] = jnp.zeros(acc_ref.shape, F32)

    def body(j, carry):
        k0 = pl.multiple_of(j * bk, bk)
        s = lax.dot_general(q2, k_ref[pl.ds(k0, bk), :], NT_DIMS, preferred_element_type=F32)
        m_prev = m_ref[...]
        m_new = jnp.maximum(m_prev, jnp.max(s, axis=1, keepdims=True))
        p = jnp.exp(s - m_new[:, :1])
        alpha = jnp.exp(m_prev - m_new)
        l_ref[...] = alpha * l_ref[...] + jnp.sum(p, axis=1, keepdims=True)
        m_ref[...] = m_new
        acc_ref[...] = alpha * acc_ref[...] + jnp.dot(
            p.astype(BF16), v_ref[pl.ds(k0, bk), :], preferred_element_type=F32)
        return carry

    lax.fori_loop(0, seq // bk, body, 0)
    o = acc_ref[...] / l_ref[...]
    low = lax.broadcasted_iota(jnp.int32, (bq, LANES), 1) < HEAD_DIM
    o_ref[...] = jnp.where(low, o[:bq], pltpu.roll(o[bq:], HEAD_DIM, 1)).astype(BF16)


def _attn_c(q, k, v, *, batch, seq, bq, bk):
    n = q.shape[0]
    nq = seq // bq
    return pl.pallas_call(
        functools.partial(_attn_c_kernel, bq=bq, bk=bk, seq=seq),
        grid=(batch, C_KV, nq),
        in_specs=[
            pl.BlockSpec((bq, 2 * LANES), lambda b, g, i: (b * nq + i, g)),
            pl.BlockSpec((seq, LANES), lambda b, g, i: (b, g)),
            pl.BlockSpec((seq, LANES), lambda b, g, i: (b, g)),
        ],
        out_specs=pl.BlockSpec((bq, LANES), lambda b, g, i: (b * nq + i, g)),
        out_shape=jax.ShapeDtypeStruct((n, C_HEADS * HEAD_DIM), BF16),
        name="attn_global",
        scratch_shapes=[
            pltpu.VMEM((2 * bq, LANES), F32),
            pltpu.VMEM((2 * bq, LANES), F32),
            pltpu.VMEM((2 * bq, LANES), F32),
        ],
        compiler_params=pltpu.CompilerParams(
            dimension_semantics=("arbitrary", "arbitrary", "arbitrary"),
            vmem_limit_bytes=VMEM_LIMIT),
    )(q, k, v)


def _out_kernel(x_ref, a_ref, b_ref, c_ref, w_ref, g_ref, o_ref):
    mix = jnp.dot(a_ref[...], w_ref[0:512, :], preferred_element_type=F32)
    mix += jnp.dot(b_ref[...], w_ref[512:768, :], preferred_element_type=F32)
    mix += jnp.dot(c_ref[...], w_ref[768:1024, :], preferred_element_type=F32)
    o_ref[...] = x_ref[...] + _rms(mix, g_ref[...])


def _out_proj(x2, a, b, c, w, g, *, bm):
    n = x2.shape[0]
    row = lambda i: (i, 0)
    fixed = lambda i: (0, 0)
    return pl.pallas_call(
        _out_kernel,
        grid=(n // bm,),
        in_specs=[
            pl.BlockSpec((bm, D_MODEL), row),
            pl.BlockSpec((bm, 512), row),
            pl.BlockSpec((bm, 256), row),
            pl.BlockSpec((bm, 256), row),
            pl.BlockSpec((D_MODEL, D_MODEL), fixed, pipeline_mode=pl.Buffered(1)),
            pl.BlockSpec((1, D_MODEL), fixed),
        ],
        out_specs=pl.BlockSpec((bm, D_MODEL), row),
        out_shape=jax.ShapeDtypeStruct((n, D_MODEL), F32),
        name="out_proj",
        compiler_params=pltpu.CompilerParams(
            dimension_semantics=("arbitrary",), vmem_limit_bytes=VMEM_LIMIT),
    )(x2, a, b, c, w, g)


def _mlp_kernel(x_ref, g1_ref, w1_ref, w2_ref, g2_ref, o_ref, *, fc):
    x = x_ref[...]
    h = _rms(x, g1_ref[...]).astype(BF16)
    y = jnp.zeros(x.shape, F32)
    for f in range(D_FF // fc):
        t = jnp.dot(h, w1_ref[:, f * fc:(f + 1) * fc], preferred_element_type=F32)
        t = jnp.square(jnp.maximum(t, 0.0)).astype(BF16)
        y += jnp.dot(t, w2_ref[f * fc:(f + 1) * fc, :], preferred_element_type=F32)
    o_ref[...] = x + _rms(y, g2_ref[...])


def _mlp(x2, g1, w1, w2, g2, *, bm, fc):
    n = x2.shape[0]
    row = lambda i: (i, 0)
    fixed = lambda i: (0, 0)
    return pl.pallas_call(
        functools.partial(_mlp_kernel, fc=fc),
        grid=(n // bm,),
        in_specs=[
            pl.BlockSpec((bm, D_MODEL), row),
            pl.BlockSpec((1, D_MODEL), fixed),
            pl.BlockSpec((D_MODEL, D_FF), fixed, pipeline_mode=pl.Buffered(1)),
            pl.BlockSpec((D_FF, D_MODEL), fixed, pipeline_mode=pl.Buffered(1)),
            pl.BlockSpec((1, D_MODEL), fixed),
        ],
        out_specs=pl.BlockSpec((bm, D_MODEL), row),
        out_shape=jax.ShapeDtypeStruct((n, D_MODEL), F32),
        name="mlp",
        compiler_params=pltpu.CompilerParams(
            dimension_semantics=("arbitrary",), vmem_limit_bytes=VMEM_LIMIT),
    )(x2, g1, w1, w2, g2)


def _rope_tables(seq):
    pos = jnp.arange(seq)
    row = (pos // GRID_W).astype(F32)
    col = (pos % GRID_W).astype(F32)
    half = HEAD_DIM // 2
    freqs = ROPE_THETA ** (-jnp.arange(0, half, 2, dtype=F32) / half)
    lane = jnp.arange(LANES)
    d = lane % HEAD_DIM
    ang = jnp.where((d < half)[None, :], row[:, None], col[:, None]) * freqs[d % 16][None, :]
    first = ((d % half) < 16)[None, :]
    cos, sin = jnp.cos(ang), jnp.sin(ang)
    return cos, jnp.where(first, -sin, 0.0), jnp.where(first, 0.0, sin)


def kernel(x, w_in, w_out, g_pre_mix, g_post_mix, lam_q1, lam_k1, lam_q2, lam_k2,
           diff_subln_g, sink_logits, c_q_norm, c_k_norm, g_pre_mlp, g_post_mlp,
           w_mlp_in, w_mlp_out):
    batch, seq, _ = x.shape
    depth = w_in.shape[0]
    n = batch * seq
    x2 = x.reshape(n, D_MODEL)
    tabs = _rope_tables(seq)
    lane = jnp.arange(LANES)
    grp = jnp.where((lane[:, None] // HEAD_DIM) == (lane[None, :] // HEAD_DIM),
                    1.0 / HEAD_DIM, 0.0).astype(F32)
    a_slopes = jnp.broadcast_to(jnp.asarray(A_SLOPES, F32)[:, None, None], (A_HEADS, 1, LANES))
    b_slopes = jnp.broadcast_to(jnp.asarray(B_SLOPES, F32)[:, None, None], (B_HEADS, 1, LANES))
    row = lambda v: v.reshape(1, -1).astype(F32)

    for l in range(depth):
        lam_init = 0.8 - 0.6 * math.exp(-0.3 * l)
        aq, ak, av, bq, bk, bv, cq, ck, cv = _proj(
            x2, row(g_pre_mix[l]), w_in[l].astype(BF16), tabs,
            row(jnp.tile(c_q_norm[l], 2)), row(jnp.tile(c_k_norm[l], 2)), grp,
            seq=seq, bm=512)
        a_out = _attn_a(a_slopes, aq, ak, av, row(lam_q1[l]), row(lam_k1[l]), row(lam_q2[l]),
                        row(lam_k2[l]), row(diff_subln_g[l]), lam_init=lam_init,
                        batch=batch, seq=seq, bq=256, bk=512)
        sinks = jnp.broadcast_to(sink_logits[l].astype(F32)[:, None, None], (B_HEADS, 1, LANES))
        b_out = _attn_b(b_slopes, sinks, bq, bk, bv, batch=batch, seq=seq, bq=WINDOW)
        c_out = _attn_c(cq, ck, cv, batch=batch, seq=seq, bq=256, bk=512)
        x2 = _out_proj(x2, a_out, b_out, c_out, w_out[l].astype(BF16), row(g_post_mix[l]), bm=512)
        x2 = _mlp(x2, row(g_pre_mlp[l]), w_mlp_in[l].astype(BF16), w_mlp_out[l].astype(BF16),
                  row(g_post_mlp[l]), bm=512, fc=1024)
    return x2.reshape(batch, seq, D_MODEL)
```

```python
import functools
import math

import jax
import jax.numpy as jnp
from jax import lax
from jax.experimental import pallas as pl
from jax.experimental.pallas import tpu as pltpu

D_MODEL = 1024
HEAD_DIM = 64
LANES = 128
A_HEADS = 4
B_HEADS = 4
B_KV = 2
C_HEADS = 4
C_KV = 2
WINDOW = 128
GRID_W = 64
ROPE_THETA = 10000.0
D_FF = 4 * D_MODEL
EPS = 1e-6
N_ALIBI = A_HEADS + B_HEADS
ALIBI_SLOPES = tuple(2.0 ** (-8.0 * (i + 1) / N_ALIBI) for i in range(N_ALIBI))
A_SLOPES = ALIBI_SLOPES[0::2]
B_SLOPES = ALIBI_SLOPES[1::2]
SCALE = HEAD_DIM ** -0.5
LOG2E = math.log2(math.e)
SCALE_LOG2E = SCALE * LOG2E
NEG_BIG = -1e30

AQ0, AK0, AV0 = 0, 512, 1024
BQ0, BK0, BV0 = 1536, 1792, 1920
CQ0, CK0, CV0 = 2048, 2304, 2432

VMEM_LIMIT = 56 * 1024 * 1024

F32 = jnp.float32
BF16 = jnp.bfloat16
NT_DIMS = (((1,), (1,)), ((), ()))


def _rms(x, g):
    ms = jnp.mean(x * x, axis=-1, keepdims=True)
    return x * lax.rsqrt(ms + EPS) * g


def _proj_kernel(x_ref, g_ref, wr_ref, wt_ref, cos_ref, s1_ref, s2_ref, gk_ref, grp_ref,
                 cost_ref, s1t_ref, s2t_ref, gqt_ref, kaug_ref, qaugt_ref,
                 aqt_ref, ak_ref, avt_ref, bq_ref, bk_ref, bv_ref, cqt_ref, ck_ref, cvt_ref):
    bm = x_ref.shape[0]
    bk = avt_ref.shape[-1]
    h = _rms(x_ref[...], g_ref[...]).astype(BF16)
    low = lax.broadcasted_iota(jnp.int32, (bm, LANES), 1) < HEAD_DIM

    def seg(c0, width):
        full = jnp.dot(h, wr_ref[:, c0:c0 + width], preferred_element_type=F32)
        return [full[:, j * LANES:(j + 1) * LANES] for j in range(width // LANES)]

    def seg_t(r0, rows):
        return lax.dot_general(wt_ref[r0:r0 + rows, :], h, NT_DIMS, preferred_element_type=F32)

    def put_slots(ref, j, blk, pad=0.0):
        ref[:, (2 * j) * LANES:(2 * j + 1) * LANES] = jnp.where(low, blk, pad).astype(BF16)
        ref[:, (2 * j + 1) * LANES:(2 * j + 2) * LANES] = jnp.where(
            low, pltpu.roll(blk, HEAD_DIM, 1), pad).astype(BF16)

    def put_slots_t(ref, xt, pads=None):
        zeros = jnp.zeros((HEAD_DIM, bm), BF16)
        for s in range(xt.shape[0] // HEAD_DIM):
            ref[s * LANES:s * LANES + HEAD_DIM, :] = xt[s * HEAD_DIM:(s + 1) * HEAD_DIM].astype(BF16)
            ref[s * LANES + HEAD_DIM:(s + 1) * LANES, :] = zeros if pads is None else pads[s]

    def put_chunks_t(ref, xt):
        for c in range(bm // bk):
            ref[c] = xt[:, c * bk:(c + 1) * bk].astype(BF16)

    def norm_rope(blk, gain):
        ms = jnp.dot(blk * blk, grp_ref[...], preferred_element_type=F32,
                     precision=lax.Precision.HIGHEST)
        y = blk * lax.rsqrt(ms + EPS) * gain
        return (y * cos_ref[...] + pltpu.roll(y, LANES - 16, 1) * s1_ref[...]
                + pltpu.roll(y, 16, 1) * s2_ref[...])

    def norm_rope_t(xt):
        ms = jnp.mean(xt * xt, axis=0, keepdims=True)
        y = xt * lax.rsqrt(ms + EPS) * jnp.tile(gqt_ref[...], (1, bm // LANES))
        up = jnp.concatenate([y[16:], y[:16]], axis=0)
        dn = jnp.concatenate([y[HEAD_DIM - 16:], y[:HEAD_DIM - 16]], axis=0)
        return y * cost_ref[...] + up * s1t_ref[...] + dn * s2t_ref[...]

    for j, blk in enumerate(seg(0, 512)):
        put_slots(ak_ref, j, blk, kaug_ref[j])
    b_blks = seg(512, 512)
    put_slots(bq_ref, 0, b_blks[0] * SCALE)
    put_slots(bq_ref, 1, b_blks[1] * SCALE)
    put_slots(bk_ref, 0, b_blks[2])
    put_slots(bv_ref, 0, b_blks[3])
    put_slots(ck_ref, 0, norm_rope(seg(1024, 128)[0], gk_ref[...]))

    put_slots_t(aqt_ref, seg_t(0, 512) * SCALE_LOG2E,
                [qaugt_ref[s // 2].astype(BF16) for s in range(2 * A_HEADS)])
    put_chunks_t(avt_ref, seg_t(512, 512))
    cqt = seg_t(1024, 256)
    cqt = jnp.concatenate([norm_rope_t(cqt[hh * HEAD_DIM:(hh + 1) * HEAD_DIM])
                           for hh in range(C_HEADS)], axis=0)
    put_slots_t(cqt_ref, cqt * SCALE_LOG2E)
    put_chunks_t(cvt_ref, seg_t(1280, 128))


def _proj(x2, g, wr, wt, tabs, tabs_t, gk, gqt, grp, kaug, qaugt, *, batch, seq, bm, bk):
    n = x2.shape[0]
    nsb = seq // bm
    cpb = bm // bk
    row = lambda i: (i, 0)
    fixed = lambda i: (0, 0)
    tab = lambda i: (i % nsb, 0)
    tab_t = lambda i: (0, i % nsb)
    tok_t = lambda i: (i // nsb, 0, i % nsb)
    chunk_t = lambda i: (i // nsb, i % nsb, 0, 0)
    out_specs = [
        pl.BlockSpec((None, 8 * LANES, bm), tok_t),
        pl.BlockSpec((bm, 8 * LANES), row),
        pl.BlockSpec((None, cpb, 512, bk), chunk_t),
        pl.BlockSpec((bm, 4 * LANES), row),
        pl.BlockSpec((bm, 2 * LANES), row),
        pl.BlockSpec((bm, 2 * LANES), row),
        pl.BlockSpec((None, 4 * LANES, bm), tok_t),
        pl.BlockSpec((bm, 2 * LANES), row),
        pl.BlockSpec((None, cpb, 128, bk), chunk_t),
    ]
    out_shape = [
        jax.ShapeDtypeStruct((batch, 8 * LANES, seq), BF16),
        jax.ShapeDtypeStruct((n, 8 * LANES), BF16),
        jax.ShapeDtypeStruct((batch, seq // bk, 512, bk), BF16),
        jax.ShapeDtypeStruct((n, 4 * LANES), BF16),
        jax.ShapeDtypeStruct((n, 2 * LANES), BF16),
        jax.ShapeDtypeStruct((n, 2 * LANES), BF16),
        jax.ShapeDtypeStruct((batch, 4 * LANES, seq), BF16),
        jax.ShapeDtypeStruct((n, 2 * LANES), BF16),
        jax.ShapeDtypeStruct((batch, seq // bk, 128, bk), BF16),
    ]
    return pl.pallas_call(
        _proj_kernel,
        grid=(n // bm,),
        in_specs=[
            pl.BlockSpec((bm, D_MODEL), row),
            pl.BlockSpec((1, D_MODEL), fixed),
            pl.BlockSpec(wr.shape, fixed, pipeline_mode=pl.Buffered(1)),
            pl.BlockSpec(wt.shape, fixed, pipeline_mode=pl.Buffered(1)),
            pl.BlockSpec((bm, LANES), tab),
            pl.BlockSpec((bm, LANES), tab),
            pl.BlockSpec((bm, LANES), tab),
            pl.BlockSpec((1, LANES), fixed),
            pl.BlockSpec((LANES, LANES), fixed),
            pl.BlockSpec((HEAD_DIM, bm), tab_t),
            pl.BlockSpec((HEAD_DIM, bm), tab_t),
            pl.BlockSpec((HEAD_DIM, bm), tab_t),
            pl.BlockSpec((HEAD_DIM, LANES), fixed),
            pl.BlockSpec((A_HEADS, bm, LANES), lambda i: (0, i % nsb, 0)),
            pl.BlockSpec((A_HEADS, HEAD_DIM, bm), lambda i: (0, 0, i % nsb)),
        ],
        out_specs=out_specs,
        out_shape=out_shape,
        name="in_proj",
        compiler_params=pltpu.CompilerParams(
            dimension_semantics=("arbitrary",), vmem_limit_bytes=VMEM_LIMIT),
    )(x2, g, wr, wt, *tabs, gk, grp, *tabs_t, gqt, kaug, qaugt)


def _attn_a_kernel(c_ref, qt_ref, k_ref, vt_ref, lq1_ref, lk1_ref, lq2_ref, lk2_ref, g_ref,
                   o_ref, acc_ref, s_ref, *, lam_init, bq, bk, seq):
    i = pl.program_id(2)
    nk = seq // bk
    jd = (i * bq) // bk
    cc = c_ref[:, :1]
    qt = qt_ref[...]
    qts = (qt[:LANES], qt[LANES:])
    aug = lax.broadcasted_iota(jnp.int32, (LANES, bq), 0) >= HEAD_DIM
    acc_ref[...] = jnp.zeros(acc_ref.shape, F32)

    def issue_scores(jj):
        j = jd + jj
        j = jnp.where(j >= nk, j - nk, j)
        k0 = pl.multiple_of(j * bk, bk)
        maxes = []
        for c in range(2):
            if jj == 0:
                qv = jnp.where(aug, jnp.zeros_like(qts[c]), qts[c])
            else:
                qv = jnp.where(jnp.logical_and(aug, j > jd), -qts[c], qts[c])
            s = jnp.dot(k_ref[pl.ds(k0, bk), c * LANES:(c + 1) * LANES], qv,
                        preferred_element_type=F32)
            if jj == 0:
                rowcol = (lax.broadcasted_iota(jnp.int32, (bk, bq), 0)
                          - lax.broadcasted_iota(jnp.int32, (bk, bq), 1)
                          + (jd * bk - i * bq)).astype(F32)
                s = s - cc * jnp.abs(rowcol)
            s_ref[jj % nslot, c] = s
            maxes.append(jnp.max(s, axis=0, keepdims=True))
        return j, maxes

    nslot = s_ref.shape[0]
    ahead = nslot - 1
    m = [jnp.full((1, bq), NEG_BIG, F32)] * 2
    l = [jnp.zeros((1, bq), F32)] * 2
    pending = [issue_scores(jj) for jj in range(min(ahead, nk))]
    for jj in range(nk):
        if jj + ahead < nk:
            pending.append(issue_scores(jj + ahead))
        j, maxes = pending.pop(0)
        vt = vt_ref[j]
        for c in range(2):
            m_new = jnp.maximum(m[c], maxes[c])
            p = jnp.exp2(s_ref[jj % nslot, c] - m_new)
            alpha = jnp.exp2(m[c] - m_new)
            l[c] = alpha * l[c] + jnp.sum(p, axis=0, keepdims=True)
            m[c] = m_new
            acc_ref[c] = alpha * acc_ref[c] + jnp.dot(
                vt, p.astype(BF16), preferred_element_type=F32)
    l1, l2 = l

    lam = (jnp.exp(jnp.sum(lq1_ref[...] * lk1_ref[...], axis=-1, keepdims=True))
           - jnp.exp(jnp.sum(lq2_ref[...] * lk2_ref[...], axis=-1, keepdims=True))
           + lam_init)
    o = (acc_ref[0] / l1 - lam * (acc_ref[1] / l2)).T
    o_ref[...] = (_rms(o, g_ref[...]) * (1.0 - lam_init)).astype(BF16)


def _attn_a(slopes, aqt, ak, avt, lq1, lk1, lq2, lk2, g, *, lam_init, batch, seq, bq, bk):
    n = ak.shape[0]
    nq = seq // bq
    vec = lambda b, h, i: (0, 0)
    kern = functools.partial(_attn_a_kernel, lam_init=lam_init, bq=bq, bk=bk, seq=seq)
    return pl.pallas_call(
        kern,
        grid=(batch, A_HEADS, nq),
        in_specs=[
            pl.BlockSpec((None, 1, LANES), lambda b, h, i: (h, 0, 0)),
            pl.BlockSpec((None, 2 * LANES, bq), lambda b, h, i: (b, h, i)),
            pl.BlockSpec((seq, 2 * LANES), lambda b, h, i: (b, h)),
            pl.BlockSpec((None, seq // bk, LANES, bk), lambda b, h, i: (b, 0, h, 0)),
            pl.BlockSpec((1, HEAD_DIM), vec),
            pl.BlockSpec((1, HEAD_DIM), vec),
            pl.BlockSpec((1, HEAD_DIM), vec),
            pl.BlockSpec((1, HEAD_DIM), vec),
            pl.BlockSpec((1, LANES), vec),
        ],
        out_specs=pl.BlockSpec((bq, LANES), lambda b, h, i: (b * nq + i, h)),
        out_shape=jax.ShapeDtypeStruct((n, A_HEADS * LANES), BF16),
        name="attn_diff",
        scratch_shapes=[pltpu.VMEM((2, LANES, bq), F32),
                        pltpu.VMEM((3, 2, bk, bq), F32)],
        compiler_params=pltpu.CompilerParams(
            dimension_semantics=("arbitrary", "arbitrary", "arbitrary"),
            vmem_limit_bytes=VMEM_LIMIT),
    )(slopes, aqt, ak, avt, lq1, lk1, lq2, lk2, g)


def _attn_b_kernel(slope_ref, sink_ref, q_ref, k_ref, v_ref, o_ref, *, bq, seq):
    i = pl.program_id(2)
    nb = seq // bq
    q = q_ref[...]
    q2 = jnp.concatenate([q[:, :LANES], q[:, LANES:]], axis=0)
    slope2 = jnp.concatenate([jnp.broadcast_to(slope_ref[0, :, :1], (bq, 1)),
                              jnp.broadcast_to(slope_ref[1, :, :1], (bq, 1))], axis=0)
    sink2 = jnp.concatenate([jnp.broadcast_to(sink_ref[0, :, :1], (bq, 1)),
                             jnp.broadcast_to(sink_ref[1, :, :1], (bq, 1))], axis=0)
    a = lax.broadcasted_iota(jnp.int32, (bq, bq), 0)
    b = lax.broadcasted_iota(jnp.int32, (bq, bq), 1)
    a = jnp.concatenate([a, a], axis=0)
    b = jnp.concatenate([b, b], axis=0)

    starts = (jnp.maximum(i - 1, 0) * bq, i * bq, jnp.minimum(i + 1, nb - 1) * bq)
    deltas = (b - bq - a, b - a, b + bq - a)
    valids = (i > 0, True, i < nb - 1)
    ss, vs = [], []
    for st, delta, ok in zip(starts, deltas, valids):
        st = pl.multiple_of(st, bq)
        kk = k_ref[pl.ds(st, bq), :]
        vs.append(v_ref[pl.ds(st, bq), :])
        s = lax.dot_general(q2, kk, NT_DIMS, preferred_element_type=F32)
        dist = jnp.abs(delta)
        s = s - slope2 * dist.astype(F32)
        mask = jnp.logical_and(dist <= WINDOW, ok)
        ss.append(jnp.where(mask, s, NEG_BIG))
    m = sink2
    for s in ss:
        m = jnp.maximum(m, jnp.max(s, axis=1, keepdims=True))
    l = jnp.exp(sink2 - m)
    acc = jnp.zeros((2 * bq, LANES), F32)
    for s, vv in zip(ss, vs):
        p = jnp.exp(s - m)
        l = l + jnp.sum(p, axis=1, keepdims=True)
        acc = acc + jnp.dot(p.astype(BF16), vv, preferred_element_type=F32)
    o = acc / l
    low = lax.broadcasted_iota(jnp.int32, (bq, LANES), 1) < HEAD_DIM
    o_ref[...] = jnp.where(low, o[:bq], pltpu.roll(o[bq:], HEAD_DIM, 1)).astype(BF16)


def _attn_b(slopes, sinks, q, k, v, *, batch, seq, bq):
    n = q.shape[0]
    nq = seq // bq
    return pl.pallas_call(
        functools.partial(_attn_b_kernel, bq=bq, seq=seq),
        grid=(batch, B_KV, nq),
        in_specs=[
            pl.BlockSpec((2, 1, LANES), lambda b, g, i: (g, 0, 0)),
            pl.BlockSpec((2, 1, LANES), lambda b, g, i: (g, 0, 0)),
            pl.BlockSpec((bq, 2 * LANES), lambda b, g, i: (b * nq + i, g)),
            pl.BlockSpec((seq, LANES), lambda b, g, i: (b, g)),
            pl.BlockSpec((seq, LANES), lambda b, g, i: (b, g)),
        ],
        out_specs=pl.BlockSpec((bq, LANES), lambda b, g, i: (b * nq + i, g)),
        out_shape=jax.ShapeDtypeStruct((n, B_HEADS * HEAD_DIM), BF16),
        name="attn_window",
        compiler_params=pltpu.CompilerParams(
            dimension_semantics=("arbitrary", "arbitrary", "arbitrary"),
            vmem_limit_bytes=VMEM_LIMIT),
    )(slopes, sinks, q, k, v)


def _attn_c_kernel(qt_ref, k_ref, vt_ref, o_ref, acc_ref, s_ref, *, bq, bk, seq):
    nk = seq // bk
    qt = qt_ref[...]
    qt2 = jnp.concatenate([qt[:LANES], qt[LANES:]], axis=1)
    acc_ref[...] = jnp.zeros(acc_ref.shape, F32)

    def issue_scores(j):
        s = jnp.dot(k_ref[j * bk:(j + 1) * bk, :], qt2, preferred_element_type=F32)
        s_ref[j % nslot] = s
        return jnp.max(s, axis=0, keepdims=True)

    nslot = s_ref.shape[0]
    ahead = nslot - 1
    m = jnp.full((1, 2 * bq), NEG_BIG, F32)
    l = jnp.zeros((1, 2 * bq), F32)
    pending = [issue_scores(j) for j in range(min(ahead, nk))]
    for j in range(nk):
        if j + ahead < nk:
            pending.append(issue_scores(j + ahead))
        mx = pending.pop(0)
        m_new = jnp.maximum(m, mx)
        p = jnp.exp2(s_ref[j % nslot] - m_new)
        alpha = jnp.exp2(m - m_new)
        l = alpha * l + jnp.sum(p, axis=0, keepdims=True)
        m = m_new
        acc_ref[...] = alpha * acc_ref[...] + jnp.dot(
            vt_ref[j], p.astype(BF16), preferred_element_type=F32)
    ot = acc_ref[...] / l
    o_ref[...] = jnp.concatenate([ot[:, :bq], ot[:, bq:]], axis=0).T.astype(BF16)


def _attn_c(cqt, ck, cvt, *, batch, seq, bq, bk):
    n = ck.shape[0]
    nq = seq // bq
    return pl.pallas_call(
        functools.partial(_attn_c_kernel, bq=bq, bk=bk, seq=seq),
        grid=(batch, C_KV, nq),
        in_specs=[
            pl.BlockSpec((None, 2 * LANES, bq), lambda b, g, i: (b, g, i)),
            pl.BlockSpec((seq, LANES), lambda b, g, i: (b, g)),
            pl.BlockSpec((None, seq // bk, HEAD_DIM, bk), lambda b, g, i: (b, 0, g, 0)),
        ],
        out_specs=pl.BlockSpec((bq, LANES), lambda b, g, i: (b * nq + i, g)),
        out_shape=jax.ShapeDtypeStruct((n, C_HEADS * HEAD_DIM), BF16),
        name="attn_global",
        scratch_shapes=[pltpu.VMEM((HEAD_DIM, 2 * bq), F32),
                        pltpu.VMEM((3, bk, 2 * bq), F32)],
        compiler_params=pltpu.CompilerParams(
            dimension_semantics=("arbitrary", "arbitrary", "arbitrary"),
            vmem_limit_bytes=VMEM_LIMIT),
    )(cqt, ck, cvt)


def _out_kernel(x_ref, a_ref, b_ref, c_ref, w_ref, g_ref, o_ref):
    mix = jnp.dot(a_ref[...], w_ref[0:512, :], preferred_element_type=F32)
    mix += jnp.dot(b_ref[...], w_ref[512:768, :], preferred_element_type=F32)
    mix += jnp.dot(c_ref[...], w_ref[768:1024, :], preferred_element_type=F32)
    o_ref[...] = x_ref[...] + _rms(mix, g_ref[...])


def _out_proj(x2, a, b, c, w, g, *, bm):
    n = x2.shape[0]
    row = lambda i: (i, 0)
    fixed = lambda i: (0, 0)
    return pl.pallas_call(
        _out_kernel,
        grid=(n // bm,),
        in_specs=[
            pl.BlockSpec((bm, D_MODEL), row),
            pl.BlockSpec((bm, 512), row),
            pl.BlockSpec((bm, 256), row),
            pl.BlockSpec((bm, 256), row),
            pl.BlockSpec((D_MODEL, D_MODEL), fixed, pipeline_mode=pl.Buffered(1)),
            pl.BlockSpec((1, D_MODEL), fixed),
        ],
        out_specs=pl.BlockSpec((bm, D_MODEL), row),
        out_shape=jax.ShapeDtypeStruct((n, D_MODEL), F32),
        name="out_proj",
        compiler_params=pltpu.CompilerParams(
            dimension_semantics=("arbitrary",), vmem_limit_bytes=VMEM_LIMIT),
    )(x2, a, b, c, w, g)


def _mlp_kernel(x_ref, g1_ref, w1_ref, w2_ref, g2_ref, o_ref, *, fc):
    x = x_ref[...]
    h = _rms(x, g1_ref[...]).astype(BF16)
    y = jnp.zeros(x.shape, F32)
    for f in range(D_FF // fc):
        t = jnp.dot(h, w1_ref[:, f * fc:(f + 1) * fc], preferred_element_type=F32)
        t = jnp.square(jnp.maximum(t, 0.0)).astype(BF16)
        y += jnp.dot(t, w2_ref[f * fc:(f + 1) * fc, :], preferred_element_type=F32)
    o_ref[...] = x + _rms(y, g2_ref[...])


def _mlp(x2, g1, w1, w2, g2, *, bm, fc):
    n = x2.shape[0]
    row = lambda i: (i, 0)
    fixed = lambda i: (0, 0)
    return pl.pallas_call(
        functools.partial(_mlp_kernel, fc=fc),
        grid=(n // bm,),
        in_specs=[
            pl.BlockSpec((bm, D_MODEL), row),
            pl.BlockSpec((1, D_MODEL), fixed),
            pl.BlockSpec((D_MODEL, D_FF), fixed, pipeline_mode=pl.Buffered(1)),
            pl.BlockSpec((D_FF, D_MODEL), fixed, pipeline_mode=pl.Buffered(1)),
            pl.BlockSpec((1, D_MODEL), fixed),
        ],
        out_specs=pl.BlockSpec((bm, D_MODEL), row),
        out_shape=jax.ShapeDtypeStruct((n, D_MODEL), F32),
        name="mlp",
        compiler_params=pltpu.CompilerParams(
            dimension_semantics=("arbitrary",), vmem_limit_bytes=VMEM_LIMIT),
    )(x2, g1, w1, w2, g2)


def _rope_tables(seq):
    pos = jnp.arange(seq)
    row = (pos // GRID_W).astype(F32)
    col = (pos % GRID_W).astype(F32)
    half = HEAD_DIM // 2
    freqs = ROPE_THETA ** (-jnp.arange(0, half, 2, dtype=F32) / half)
    d = jnp.arange(HEAD_DIM)
    ang = jnp.where((d < half)[None, :], row[:, None], col[:, None]) * freqs[d % 16][None, :]
    first = ((d % half) < 16)[None, :]
    cos, sin = jnp.cos(ang), jnp.sin(ang)
    return cos, jnp.where(first, -sin, 0.0), jnp.where(first, 0.0, sin)


def _alibi_tables(seq):
    pos = jnp.arange(seq)
    parts = [((pos // 64) * 64).astype(F32), (pos % 64).astype(F32)]
    pos4 = [jnp.broadcast_to(p[None, :], (A_HEADS, seq)) for p in parts * 2]
    c = jnp.asarray(A_SLOPES, F32) * LOG2E
    c_hi = c.astype(BF16).astype(F32)
    c_lo = (c - c_hi).astype(BF16).astype(F32)
    c4 = [jnp.broadcast_to(v[:, None], (A_HEADS, seq)) for v in (c_hi, c_hi, c_lo, c_lo)]
    kaug = jnp.zeros((A_HEADS, seq, LANES), F32).at[:, :, HEAD_DIM:HEAD_DIM + 8].set(
        jnp.stack([-v for v in c4] + pos4, axis=-1))
    qaugt = jnp.zeros((A_HEADS, HEAD_DIM, seq), F32).at[:, 0:8, :].set(
        jnp.stack(pos4 + c4, axis=1))
    c_tab = jnp.broadcast_to(c[:, None, None], (A_HEADS, 1, LANES))
    return kaug, qaugt, c_tab


def kernel(x, w_in, w_out, g_pre_mix, g_post_mix, lam_q1, lam_k1, lam_q2, lam_k2,
           diff_subln_g, sink_logits, c_q_norm, c_k_norm, g_pre_mlp, g_post_mlp,
           w_mlp_in, w_mlp_out):
    batch, seq, _ = x.shape
    depth = w_in.shape[0]
    n = batch * seq
    x2 = x.reshape(n, D_MODEL)
    tabs64 = _rope_tables(seq)
    tabs = tuple(jnp.tile(t, (1, 2)) for t in tabs64)
    tabs_t = tuple(t.T for t in tabs64)
    lane = jnp.arange(LANES)
    grp = jnp.where((lane[:, None] // HEAD_DIM) == (lane[None, :] // HEAD_DIM),
                    1.0 / HEAD_DIM, 0.0).astype(F32)
    kaug, qaugt, a_ctab = _alibi_tables(seq)
    b_slopes = jnp.broadcast_to(jnp.asarray(B_SLOPES, F32)[:, None, None], (B_HEADS, 1, LANES))
    row = lambda v: v.reshape(1, -1).astype(F32)
    bk = 512

    for l in range(depth):
        lam_init = 0.8 - 0.6 * math.exp(-0.3 * l)
        w = w_in[l].astype(BF16)
        wr = jnp.concatenate([w[:, AK0:AV0], w[:, BQ0:CQ0], w[:, CK0:CV0]], axis=1)
        wt = jnp.concatenate([w[:, AQ0:AK0], w[:, AV0:BQ0], w[:, CQ0:CK0], w[:, CV0:]], axis=1).T
        gqt = jnp.broadcast_to(c_q_norm[l].astype(F32)[:, None], (HEAD_DIM, LANES))
        aqt, ak, avt, bq, bkk, bv, cqt, ck, cvt = _proj(
            x2, row(g_pre_mix[l]), wr, wt, tabs, tabs_t, row(jnp.tile(c_k_norm[l], 2)), gqt, grp,
            kaug, qaugt, batch=batch, seq=seq, bm=512, bk=bk)
        a_out = _attn_a(a_ctab, aqt, ak, avt, row(lam_q1[l]), row(lam_k1[l]), row(lam_q2[l]),
                        row(lam_k2[l]), row(diff_subln_g[l]), lam_init=lam_init,
                        batch=batch, seq=seq, bq=256, bk=bk)
        sinks = jnp.broadcast_to(sink_logits[l].astype(F32)[:, None, None], (B_HEADS, 1, LANES))
        b_out = _attn_b(b_slopes, sinks, bq, bkk, bv, batch=batch, seq=seq, bq=WINDOW)
        c_out = _attn_c(cqt, ck, cvt, batch=batch, seq=seq, bq=256, bk=bk)
        x2 = _out_proj(x2, a_out, b_out, c_out, w_out[l].astype(BF16), row(g_post_mix[l]), bm=512)
        x2 = _mlp(x2, row(g_pre_mlp[l]), w_mlp_in[l].astype(BF16), w_mlp_out[l].astype(BF16),
                  row(g_post_mlp[l]), bm=512, fc=1024)
    return x2.reshape(batch, seq, D_MODEL)
```

```python
import functools
import math

import jax
import jax.numpy as jnp
from jax import lax
from jax.experimental import pallas as pl
from jax.experimental.pallas import tpu as pltpu

D_MODEL = 1024
HEAD_DIM = 64
LANES = 128
A_HEADS = 4
B_HEADS = 4
B_KV = 2
C_HEADS = 4
C_KV = 2
WINDOW = 128
GRID_W = 64
ROPE_THETA = 10000.0
D_FF = 4 * D_MODEL
EPS = 1e-6
N_ALIBI = A_HEADS + B_HEADS
ALIBI_SLOPES = tuple(2.0 ** (-8.0 * (i + 1) / N_ALIBI) for i in range(N_ALIBI))
A_SLOPES = ALIBI_SLOPES[0::2]
B_SLOPES = ALIBI_SLOPES[1::2]
SCALE = HEAD_DIM ** -0.5
LOG2E = math.log2(math.e)
SCALE_LOG2E = SCALE * LOG2E
NEG_BIG = -1e30

AQ0, AK0, AV0 = 0, 512, 1024
BQ0, BK0, BV0 = 1536, 1792, 1920
CQ0, CK0, CV0 = 2048, 2304, 2432

VMEM_LIMIT = 56 * 1024 * 1024

ROW_BLOCK = 512
KEY_CHUNK = 512
A_QUERY_BLOCK = 256
C_QUERY_BLOCK = 256
SUB_KEYS = 128
SUM_ROWS = 16
A_VT_ROWS = LANES + SUM_ROWS
C_VT_ROWS = HEAD_DIM + SUM_ROWS
A_HEADS_PER_STEP = 2
A_SCORE_SLOTS = 3
C_SCORE_SLOTS = 3

F32 = jnp.float32
BF16 = jnp.bfloat16
NT_DIMS = (((1,), (1,)), ((), ()))


def _rms(x, g):
    ms = jnp.mean(x * x, axis=-1, keepdims=True)
    return x * lax.rsqrt(ms + EPS) * g


def _proj_kernel(x_ref, g_ref, wr_ref, wt_ref, cos_ref, s1_ref, s2_ref, gk_ref, grp_ref,
                 cost_ref, s1t_ref, s2t_ref, gqt_ref, kaug_ref, qaugt_ref,
                 aqt_ref, ak_ref, avt_ref, bq_ref, bk_ref, bv_ref, cqt_ref, ck_ref, cvt_ref):
    bm = x_ref.shape[0]
    bk = avt_ref.shape[-1]
    h = _rms(x_ref[...], g_ref[...]).astype(BF16)
    low = lax.broadcasted_iota(jnp.int32, (bm, LANES), 1) < HEAD_DIM

    def seg(c0, width):
        full = jnp.dot(h, wr_ref[:, c0:c0 + width], preferred_element_type=F32)
        return [full[:, j * LANES:(j + 1) * LANES] for j in range(width // LANES)]

    def seg_t(r0, rows):
        return lax.dot_general(wt_ref[r0:r0 + rows, :], h, NT_DIMS, preferred_element_type=F32)

    def put_slots(ref, j, blk, pad=0.0):
        ref[:, (2 * j) * LANES:(2 * j + 1) * LANES] = jnp.where(low, blk, pad).astype(BF16)
        ref[:, (2 * j + 1) * LANES:(2 * j + 2) * LANES] = jnp.where(
            low, pltpu.roll(blk, HEAD_DIM, 1), pad).astype(BF16)

    def put_slots_t(ref, xt, pads=None):
        zeros = jnp.zeros((HEAD_DIM, bm), BF16)
        for s in range(xt.shape[0] // HEAD_DIM):
            ref[s * LANES:s * LANES + HEAD_DIM, :] = xt[s * HEAD_DIM:(s + 1) * HEAD_DIM].astype(BF16)
            ref[s * LANES + HEAD_DIM:(s + 1) * LANES, :] = zeros if pads is None else pads[s]

    def put_chunks_t(ref, xt, width):
        ones_row = jnp.where(lax.broadcasted_iota(jnp.int32, (SUM_ROWS, bk), 0) == 0,
                             1.0, 0.0).astype(BF16)
        ext = width + SUM_ROWS
        for c in range(bm // bk):
            for hh in range(xt.shape[0] // width):
                ref[c, hh * ext:hh * ext + width, :] = xt[hh * width:(hh + 1) * width,
                                                          c * bk:(c + 1) * bk].astype(BF16)
                ref[c, hh * ext + width:(hh + 1) * ext, :] = ones_row

    def norm_rope(blk, gain):
        ms = jnp.dot(blk * blk, grp_ref[...], preferred_element_type=F32,
                     precision=lax.Precision.HIGHEST)
        y = blk * lax.rsqrt(ms + EPS) * gain
        return (y * cos_ref[...] + pltpu.roll(y, LANES - 16, 1) * s1_ref[...]
                + pltpu.roll(y, 16, 1) * s2_ref[...])

    def norm_rope_t(xt):
        ms = jnp.mean(xt * xt, axis=0, keepdims=True)
        y = xt * lax.rsqrt(ms + EPS) * jnp.tile(gqt_ref[...], (1, bm // LANES))
        up = jnp.concatenate([y[16:], y[:16]], axis=0)
        dn = jnp.concatenate([y[HEAD_DIM - 16:], y[:HEAD_DIM - 16]], axis=0)
        return y * cost_ref[...] + up * s1t_ref[...] + dn * s2t_ref[...]

    for j, blk in enumerate(seg(0, 512)):
        put_slots(ak_ref, j, blk, kaug_ref[j])
    b_blks = seg(512, 512)
    put_slots(bq_ref, 0, b_blks[0] * SCALE)
    put_slots(bq_ref, 1, b_blks[1] * SCALE)
    put_slots(bk_ref, 0, b_blks[2])
    put_slots(bv_ref, 0, b_blks[3])
    put_slots(ck_ref, 0, norm_rope(seg(1024, 128)[0], gk_ref[...]))

    put_slots_t(aqt_ref, seg_t(0, 512) * SCALE_LOG2E,
                [qaugt_ref[s // 2].astype(BF16) for s in range(2 * A_HEADS)])
    put_chunks_t(avt_ref, seg_t(512, 512), LANES)
    cqt = seg_t(1024, 256)
    cqt = jnp.concatenate([norm_rope_t(cqt[hh * HEAD_DIM:(hh + 1) * HEAD_DIM])
                           for hh in range(C_HEADS)], axis=0)
    put_slots_t(cqt_ref, cqt * SCALE_LOG2E)
    put_chunks_t(cvt_ref, seg_t(1280, 128), HEAD_DIM)


def _proj(x2, g, wr, wt, tabs, tabs_t, gk, gqt, grp, kaug, qaugt, *, batch, seq, bm, bk):
    n = x2.shape[0]
    nsb = seq // bm
    cpb = bm // bk
    row = lambda i: (i, 0)
    fixed = lambda i: (0, 0)
    tab = lambda i: (i % nsb, 0)
    tab_t = lambda i: (0, i % nsb)
    tok_t = lambda i: (i // nsb, 0, i % nsb)
    chunk_t = lambda i: (i // nsb, i % nsb, 0, 0)
    out_specs = [
        pl.BlockSpec((None, 8 * LANES, bm), tok_t),
        pl.BlockSpec((bm, 8 * LANES), row),
        pl.BlockSpec((None, cpb, A_HEADS * A_VT_ROWS, bk), chunk_t),
        pl.BlockSpec((bm, 4 * LANES), row),
        pl.BlockSpec((bm, 2 * LANES), row),
        pl.BlockSpec((bm, 2 * LANES), row),
        pl.BlockSpec((None, 4 * LANES, bm), tok_t),
        pl.BlockSpec((bm, 2 * LANES), row),
        pl.BlockSpec((None, cpb, C_KV * C_VT_ROWS, bk), chunk_t),
    ]
    out_shape = [
        jax.ShapeDtypeStruct((batch, 8 * LANES, seq), BF16),
        jax.ShapeDtypeStruct((n, 8 * LANES), BF16),
        jax.ShapeDtypeStruct((batch, seq // bk, A_HEADS * A_VT_ROWS, bk), BF16),
        jax.ShapeDtypeStruct((n, 4 * LANES), BF16),
        jax.ShapeDtypeStruct((n, 2 * LANES), BF16),
        jax.ShapeDtypeStruct((n, 2 * LANES), BF16),
        jax.ShapeDtypeStruct((batch, 4 * LANES, seq), BF16),
        jax.ShapeDtypeStruct((n, 2 * LANES), BF16),
        jax.ShapeDtypeStruct((batch, seq // bk, C_KV * C_VT_ROWS, bk), BF16),
    ]
    return pl.pallas_call(
        _proj_kernel,
        grid=(n // bm,),
        in_specs=[
            pl.BlockSpec((bm, D_MODEL), row),
            pl.BlockSpec((1, D_MODEL), fixed),
            pl.BlockSpec(wr.shape, fixed, pipeline_mode=pl.Buffered(1)),
            pl.BlockSpec(wt.shape, fixed, pipeline_mode=pl.Buffered(1)),
            pl.BlockSpec((bm, LANES), tab),
            pl.BlockSpec((bm, LANES), tab),
            pl.BlockSpec((bm, LANES), tab),
            pl.BlockSpec((1, LANES), fixed),
            pl.BlockSpec((LANES, LANES), fixed),
            pl.BlockSpec((HEAD_DIM, bm), tab_t),
            pl.BlockSpec((HEAD_DIM, bm), tab_t),
            pl.BlockSpec((HEAD_DIM, bm), tab_t),
            pl.BlockSpec((HEAD_DIM, LANES), fixed),
            pl.BlockSpec((A_HEADS, bm, LANES), lambda i: (0, i % nsb, 0)),
            pl.BlockSpec((A_HEADS, HEAD_DIM, bm), lambda i: (0, 0, i % nsb)),
        ],
        out_specs=out_specs,
        out_shape=out_shape,
        name="in_proj",
        compiler_params=pltpu.CompilerParams(
            dimension_semantics=("arbitrary",), vmem_limit_bytes=VMEM_LIMIT),
    )(x2, g, wr, wt, *tabs, gk, grp, *tabs_t, gqt, kaug, qaugt)


def _fold8(x, op):
    out = x[0:8]
    for r in range(8, x.shape[0], 8):
        out = op(out, x[r:r + 8])
    return out


def _attn_a_kernel(c_ref, qt_ref, k_ref, vt_ref, lam_ref, g_ref,
                   o_ref, acc_ref, s_ref, p_ref, *, lam_init, bq, bk, seq):
    i = pl.program_id(2)
    nk = seq // bk
    nstream = acc_ref.shape[0]
    ndiag = max(1, bq // bk)
    jd = (i * bq) // bk
    aug = lax.broadcasted_iota(jnp.int32, (LANES, bq), 0) >= HEAD_DIM
    qts = [qt_ref[t * LANES:(t + 1) * LANES, :] for t in range(nstream)]
    acc_ref[...] = jnp.zeros(acc_ref.shape, F32)

    def issue_scores(jj):
        j = jd + jj
        j = jnp.where(j >= nk, j - nk, j)
        k0 = pl.multiple_of(j * bk, bk)
        maxes = []
        for t in range(nstream):
            if jj < ndiag:
                qv = jnp.where(aug, jnp.zeros_like(qts[t]), qts[t])
            else:
                qv = jnp.where(jnp.logical_and(aug, j > jd), -qts[t], qts[t])
            mx = None
            for r in range(bk // SUB_KEYS):
                r0 = r * SUB_KEYS
                s = jnp.dot(k_ref[pl.ds(k0 + r0, SUB_KEYS), t * LANES:(t + 1) * LANES], qv,
                            preferred_element_type=F32)
                if jj < ndiag:
                    rowcol = (lax.broadcasted_iota(jnp.int32, (SUB_KEYS, bq), 0)
                              - lax.broadcasted_iota(jnp.int32, (SUB_KEYS, bq), 1)
                              + (j * bk + r0 - i * bq)).astype(F32)
                    s = s - c_ref[t // 2, :, :1] * jnp.abs(rowcol)
                s_ref[jj % nslot, t, r0:r0 + SUB_KEYS, :] = s
                pm = _fold8(s, jnp.maximum)
                mx = pm if mx is None else jnp.maximum(mx, pm)
            maxes.append(jnp.max(mx, axis=0, keepdims=True))
        return j, maxes

    nslot = s_ref.shape[0]
    ahead = nslot - 1
    m = [jnp.full((1, bq), NEG_BIG, F32)] * nstream
    pending = [issue_scores(jj) for jj in range(min(ahead, nk))]
    for jj in range(nk):
        if jj + ahead < nk:
            pending.append(issue_scores(jj + ahead))
        j, maxes = pending.pop(0)
        for t in range(nstream):
            m_new = jnp.maximum(m[t], maxes[t])
            alpha = jnp.exp2(m[t] - m_new)
            m[t] = m_new
            for r in range(bk // SUB_KEYS):
                r0 = r * SUB_KEYS
                p = jnp.exp2(s_ref[jj % nslot, t, r0:r0 + SUB_KEYS, :] - m_new)
                p_ref[jj % 2, t, r0:r0 + SUB_KEYS, :] = p.astype(BF16)
            acc_ref[t] = alpha * acc_ref[t] + jnp.dot(
                vt_ref[j, (t // 2) * A_VT_ROWS:(t // 2 + 1) * A_VT_ROWS, :], p_ref[jj % 2, t],
                preferred_element_type=F32)

    lam = (jnp.exp(jnp.sum(lam_ref[0:1] * lam_ref[1:2], axis=-1, keepdims=True))
           - jnp.exp(jnp.sum(lam_ref[2:3] * lam_ref[3:4], axis=-1, keepdims=True))
           + lam_init)

    def normalised(t):
        return acc_ref[t, 0:LANES, :] / acc_ref[t, LANES:LANES + 1, :]

    for hh in range(nstream // 2):
        o = (normalised(2 * hh) - lam * normalised(2 * hh + 1)).T
        o_ref[:, hh * LANES:(hh + 1) * LANES] = (
            _rms(o, g_ref[...]) * (1.0 - lam_init)).astype(BF16)


def _attn_a(ctab, aqt, ak, avt, lam_vecs, g, *, lam_init, batch, seq, bq, bk, hps, nslot):
    n = ak.shape[0]
    nq = seq // bq
    vec = lambda b, h, i: (0, 0)
    kern = functools.partial(_attn_a_kernel, lam_init=lam_init, bq=bq, bk=bk, seq=seq)
    return pl.pallas_call(
        kern,
        grid=(batch, A_HEADS // hps, nq),
        in_specs=[
            pl.BlockSpec((hps, 1, LANES), lambda b, h, i: (h, 0, 0)),
            pl.BlockSpec((None, hps * 2 * LANES, bq), lambda b, h, i: (b, h, i)),
            pl.BlockSpec((seq, hps * 2 * LANES), lambda b, h, i: (b, h)),
            pl.BlockSpec((None, seq // bk, hps * A_VT_ROWS, bk), lambda b, h, i: (b, 0, h, 0)),
            pl.BlockSpec((4, HEAD_DIM), vec),
            pl.BlockSpec((1, LANES), vec),
        ],
        out_specs=pl.BlockSpec((bq, hps * LANES), lambda b, h, i: (b * nq + i, h)),
        out_shape=jax.ShapeDtypeStruct((n, A_HEADS * LANES), BF16),
        name="attn_diff",
        scratch_shapes=[pltpu.VMEM((2 * hps, A_VT_ROWS, bq), F32),
                        pltpu.VMEM((nslot, 2 * hps, bk, bq), F32),
                        pltpu.VMEM((2, 2 * hps, bk, bq), BF16)],
        compiler_params=pltpu.CompilerParams(
            dimension_semantics=("arbitrary", "arbitrary", "arbitrary"),
            vmem_limit_bytes=VMEM_LIMIT),
    )(ctab, aqt, ak, avt, lam_vecs, g)


def _attn_b_kernel(slope_ref, sink_ref, q_ref, k_ref, v_ref, o_ref, *, bq, seq):
    i = pl.program_id(2)
    nb = seq // bq
    q = q_ref[...]
    q2 = jnp.concatenate([q[:, :LANES], q[:, LANES:]], axis=0)
    slope2 = jnp.concatenate([jnp.broadcast_to(slope_ref[0, :, :1], (bq, 1)),
                              jnp.broadcast_to(slope_ref[1, :, :1], (bq, 1))], axis=0)
    sink2 = jnp.concatenate([jnp.broadcast_to(sink_ref[0, :, :1], (bq, 1)),
                             jnp.broadcast_to(sink_ref[1, :, :1], (bq, 1))], axis=0)
    a = lax.broadcasted_iota(jnp.int32, (bq, bq), 0)
    b = lax.broadcasted_iota(jnp.int32, (bq, bq), 1)
    a = jnp.concatenate([a, a], axis=0)
    b = jnp.concatenate([b, b], axis=0)

    starts = (jnp.maximum(i - 1, 0) * bq, i * bq, jnp.minimum(i + 1, nb - 1) * bq)
    deltas = (b - bq - a, b - a, b + bq - a)
    valids = (i > 0, True, i < nb - 1)
    ss, vs = [], []
    for st, delta, ok in zip(starts, deltas, valids):
        st = pl.multiple_of(st, bq)
        kk = k_ref[pl.ds(st, bq), :]
        vs.append(v_ref[pl.ds(st, bq), :])
        s = lax.dot_general(q2, kk, NT_DIMS, preferred_element_type=F32)
        dist = jnp.abs(delta)
        s = s - slope2 * dist.astype(F32)
        mask = jnp.logical_and(dist <= WINDOW, ok)
        ss.append(jnp.where(mask, s, NEG_BIG))
    m = sink2
    for s in ss:
        m = jnp.maximum(m, jnp.max(s, axis=1, keepdims=True))
    l = jnp.exp(sink2 - m)
    acc = jnp.zeros((2 * bq, LANES), F32)
    for s, vv in zip(ss, vs):
        p = jnp.exp(s - m)
        l = l + jnp.sum(p, axis=1, keepdims=True)
        acc = acc + jnp.dot(p.astype(BF16), vv, preferred_element_type=F32)
    o = acc / l
    low = lax.broadcasted_iota(jnp.int32, (bq, LANES), 1) < HEAD_DIM
    o_ref[...] = jnp.where(low, o[:bq], pltpu.roll(o[bq:], HEAD_DIM, 1)).astype(BF16)


def _attn_b(slopes, sinks, q, k, v, *, batch, seq, bq):
    n = q.shape[0]
    nq = seq // bq
    return pl.pallas_call(
        functools.partial(_attn_b_kernel, bq=bq, seq=seq),
        grid=(batch, B_KV, nq),
        in_specs=[
            pl.BlockSpec((2, 1, LANES), lambda b, g, i: (g, 0, 0)),
            pl.BlockSpec((2, 1, LANES), lambda b, g, i: (g, 0, 0)),
            pl.BlockSpec((bq, 2 * LANES), lambda b, g, i: (b * nq + i, g)),
            pl.BlockSpec((seq, LANES), lambda b, g, i: (b, g)),
            pl.BlockSpec((seq, LANES), lambda b, g, i: (b, g)),
        ],
        out_specs=pl.BlockSpec((bq, LANES), lambda b, g, i: (b * nq + i, g)),
        out_shape=jax.ShapeDtypeStruct((n, B_HEADS * HEAD_DIM), BF16),
        name="attn_window",
        compiler_params=pltpu.CompilerParams(
            dimension_semantics=("arbitrary", "arbitrary", "arbitrary"),
            vmem_limit_bytes=VMEM_LIMIT),
    )(slopes, sinks, q, k, v)


def _attn_c_kernel(qt_ref, k_ref, vt_ref, o_ref, acc_ref, s_ref, p_ref, *, bq, bk, seq):
    nk = seq // bk
    nstream = acc_ref.shape[0]
    qts = [qt_ref[t * LANES:(t + 1) * LANES, :] for t in range(nstream)]
    acc_ref[...] = jnp.zeros(acc_ref.shape, F32)

    def issue_scores(j):
        maxes = []
        for t in range(nstream):
            mx = None
            for r in range(bk // SUB_KEYS):
                r0 = j * bk + r * SUB_KEYS
                s = jnp.dot(k_ref[r0:r0 + SUB_KEYS, :], qts[t], preferred_element_type=F32)
                s_ref[j % nslot, t, r * SUB_KEYS:(r + 1) * SUB_KEYS, :] = s
                pm = _fold8(s, jnp.maximum)
                mx = pm if mx is None else jnp.maximum(mx, pm)
            maxes.append(jnp.max(mx, axis=0, keepdims=True))
        return maxes

    nslot = s_ref.shape[0]
    ahead = nslot - 1
    m = [jnp.full((1, bq), NEG_BIG, F32)] * nstream
    pending = [issue_scores(j) for j in range(min(ahead, nk))]
    for j in range(nk):
        if j + ahead < nk:
            pending.append(issue_scores(j + ahead))
        maxes = pending.pop(0)
        for t in range(nstream):
            m_new = jnp.maximum(m[t], maxes[t])
            alpha = jnp.exp2(m[t] - m_new)
            m[t] = m_new
            for r in range(bk // SUB_KEYS):
                r0 = r * SUB_KEYS
                p = jnp.exp2(s_ref[j % nslot, t, r0:r0 + SUB_KEYS, :] - m_new)
                p_ref[j % 2, t, r0:r0 + SUB_KEYS, :] = p.astype(BF16)
            acc_ref[t] = alpha * acc_ref[t] + jnp.dot(
                vt_ref[j], p_ref[j % 2, t], preferred_element_type=F32)
    ot = jnp.concatenate([acc_ref[t, 0:HEAD_DIM, :] / acc_ref[t, HEAD_DIM:HEAD_DIM + 1, :]
                          for t in range(nstream)], axis=0)
    o_ref[...] = ot.T.astype(BF16)


def _attn_c(cqt, ck, cvt, *, batch, seq, bq, bk):
    n = ck.shape[0]
    nq = seq // bq
    return pl.pallas_call(
        functools.partial(_attn_c_kernel, bq=bq, bk=bk, seq=seq),
        grid=(batch, C_KV, nq),
        in_specs=[
            pl.BlockSpec((None, 2 * LANES, bq), lambda b, g, i: (b, g, i)),
            pl.BlockSpec((seq, LANES), lambda b, g, i: (b, g)),
            pl.BlockSpec((None, seq // bk, C_VT_ROWS, bk), lambda b, g, i: (b, 0, g, 0)),
        ],
        out_specs=pl.BlockSpec((bq, LANES), lambda b, g, i: (b * nq + i, g)),
        out_shape=jax.ShapeDtypeStruct((n, C_HEADS * HEAD_DIM), BF16),
        name="attn_global",
        scratch_shapes=[pltpu.VMEM((2, C_VT_ROWS, bq), F32),
                        pltpu.VMEM((C_SCORE_SLOTS, 2, bk, bq), F32),
                        pltpu.VMEM((2, 2, bk, bq), BF16)],
        compiler_params=pltpu.CompilerParams(
            dimension_semantics=("arbitrary", "arbitrary", "arbitrary"),
            vmem_limit_bytes=VMEM_LIMIT),
    )(cqt, ck, cvt)


def _out_kernel(x_ref, a_ref, b_ref, c_ref, w_ref, g_ref, o_ref):
    mix = jnp.dot(a_ref[...], w_ref[0:512, :], preferred_element_type=F32)
    mix += jnp.dot(b_ref[...], w_ref[512:768, :], preferred_element_type=F32)
    mix += jnp.dot(c_ref[...], w_ref[768:1024, :], preferred_element_type=F32)
    o_ref[...] = x_ref[...] + _rms(mix, g_ref[...])


def _out_proj(x2, a, b, c, w, g, *, bm):
    n = x2.shape[0]
    row = lambda i: (i, 0)
    fixed = lambda i: (0, 0)
    return pl.pallas_call(
        _out_kernel,
        grid=(n // bm,),
        in_specs=[
            pl.BlockSpec((bm, D_MODEL), row),
            pl.BlockSpec((bm, 512), row),
            pl.BlockSpec((bm, 256), row),
            pl.BlockSpec((bm, 256), row),
            pl.BlockSpec((D_MODEL, D_MODEL), fixed, pipeline_mode=pl.Buffered(1)),
            pl.BlockSpec((1, D_MODEL), fixed),
        ],
        out_specs=pl.BlockSpec((bm, D_MODEL), row),
        out_shape=jax.ShapeDtypeStruct((n, D_MODEL), F32),
        name="out_proj",
        compiler_params=pltpu.CompilerParams(
            dimension_semantics=("arbitrary",), vmem_limit_bytes=VMEM_LIMIT),
    )(x2, a, b, c, w, g)


def _mlp_kernel(x_ref, g1_ref, w1_ref, w2_ref, g2_ref, o_ref, *, fc):
    x = x_ref[...]
    h = _rms(x, g1_ref[...]).astype(BF16)
    y = jnp.zeros(x.shape, F32)
    for f in range(D_FF // fc):
        t = jnp.dot(h, w1_ref[:, f * fc:(f + 1) * fc], preferred_element_type=F32)
        t = jnp.square(jnp.maximum(t, 0.0)).astype(BF16)
        y += jnp.dot(t, w2_ref[f * fc:(f + 1) * fc, :], preferred_element_type=F32)
    o_ref[...] = x + _rms(y, g2_ref[...])


def _mlp(x2, g1, w1, w2, g2, *, bm, fc):
    n = x2.shape[0]
    row = lambda i: (i, 0)
    fixed = lambda i: (0, 0)
    return pl.pallas_call(
        functools.partial(_mlp_kernel, fc=fc),
        grid=(n // bm,),
        in_specs=[
            pl.BlockSpec((bm, D_MODEL), row),
            pl.BlockSpec((1, D_MODEL), fixed),
            pl.BlockSpec((D_MODEL, D_FF), fixed, pipeline_mode=pl.Buffered(1)),
            pl.BlockSpec((D_FF, D_MODEL), fixed, pipeline_mode=pl.Buffered(1)),
            pl.BlockSpec((1, D_MODEL), fixed),
        ],
        out_specs=pl.BlockSpec((bm, D_MODEL), row),
        out_shape=jax.ShapeDtypeStruct((n, D_MODEL), F32),
        name="mlp",
        compiler_params=pltpu.CompilerParams(
            dimension_semantics=("arbitrary",), vmem_limit_bytes=VMEM_LIMIT),
    )(x2, g1, w1, w2, g2)


def _rope_tables(seq):
    pos = jnp.arange(seq)
    row = (pos // GRID_W).astype(F32)
    col = (pos % GRID_W).astype(F32)
    half = HEAD_DIM // 2
    freqs = ROPE_THETA ** (-jnp.arange(0, half, 2, dtype=F32) / half)
    d = jnp.arange(HEAD_DIM)
    ang = jnp.where((d < half)[None, :], row[:, None], col[:, None]) * freqs[d % 16][None, :]
    first = ((d % half) < 16)[None, :]
    cos, sin = jnp.cos(ang), jnp.sin(ang)
    return cos, jnp.where(first, -sin, 0.0), jnp.where(first, 0.0, sin)


def _alibi_tables(seq):
    pos = jnp.arange(seq)
    parts = [((pos // 64) * 64).astype(F32), (pos % 64).astype(F32)]
    pos4 = [jnp.broadcast_to(p[None, :], (A_HEADS, seq)) for p in parts * 2]
    c = jnp.asarray(A_SLOPES, F32) * LOG2E
    c_hi = c.astype(BF16).astype(F32)
    c_lo = (c - c_hi).astype(BF16).astype(F32)
    c4 = [jnp.broadcast_to(v[:, None], (A_HEADS, seq)) for v in (c_hi, c_hi, c_lo, c_lo)]
    kaug = jnp.zeros((A_HEADS, seq, LANES), F32).at[:, :, HEAD_DIM:HEAD_DIM + 8].set(
        jnp.stack([-v for v in c4] + pos4, axis=-1))
    qaugt = jnp.zeros((A_HEADS, HEAD_DIM, seq), F32).at[:, 0:8, :].set(
        jnp.stack(pos4 + c4, axis=1))
    c_tab = jnp.broadcast_to(c[:, None, None], (A_HEADS, 1, LANES))
    return kaug, qaugt, c_tab


def kernel(x, w_in, w_out, g_pre_mix, g_post_mix, lam_q1, lam_k1, lam_q2, lam_k2,
           diff_subln_g, sink_logits, c_q_norm, c_k_norm, g_pre_mlp, g_post_mlp,
           w_mlp_in, w_mlp_out):
    batch, seq, _ = x.shape
    depth = w_in.shape[0]
    n = batch * seq
    x2 = x.reshape(n, D_MODEL)
    tabs64 = _rope_tables(seq)
    tabs = tuple(jnp.tile(t, (1, 2)) for t in tabs64)
    tabs_t = tuple(t.T for t in tabs64)
    lane = jnp.arange(LANES)
    grp = jnp.where((lane[:, None] // HEAD_DIM) == (lane[None, :] // HEAD_DIM),
                    1.0 / HEAD_DIM, 0.0).astype(F32)
    kaug, qaugt, a_ctab = _alibi_tables(seq)
    b_slopes = jnp.broadcast_to(jnp.asarray(B_SLOPES, F32)[:, None, None], (B_HEADS, 1, LANES))
    row = lambda v: v.reshape(1, -1).astype(F32)
    bk = KEY_CHUNK

    for l in range(depth):
        lam_init = 0.8 - 0.6 * math.exp(-0.3 * l)
        w = w_in[l].astype(BF16)
        wr = jnp.concatenate([w[:, AK0:AV0], w[:, BQ0:CQ0], w[:, CK0:CV0]], axis=1)
        wt = jnp.concatenate([w[:, AQ0:AK0], w[:, AV0:BQ0], w[:, CQ0:CK0], w[:, CV0:]], axis=1).T
        gqt = jnp.broadcast_to(c_q_norm[l].astype(F32)[:, None], (HEAD_DIM, LANES))
        aqt, ak, avt, bq, bkk, bv, cqt, ck, cvt = _proj(
            x2, row(g_pre_mix[l]), wr, wt, tabs, tabs_t, row(jnp.tile(c_k_norm[l], 2)), gqt, grp,
            kaug, qaugt, batch=batch, seq=seq, bm=ROW_BLOCK, bk=bk)
        lam_vecs = jnp.stack([lam_q1[l], lam_k1[l], lam_q2[l], lam_k2[l]]).astype(F32)
        a_out = _attn_a(a_ctab, aqt, ak, avt, lam_vecs, row(diff_subln_g[l]), lam_init=lam_init,
                        batch=batch, seq=seq, bq=A_QUERY_BLOCK, bk=bk,
                        hps=A_HEADS_PER_STEP, nslot=A_SCORE_SLOTS)
        sinks = jnp.broadcast_to(sink_logits[l].astype(F32)[:, None, None], (B_HEADS, 1, LANES))
        b_out = _attn_b(b_slopes, sinks, bq, bkk, bv, batch=batch, seq=seq, bq=WINDOW)
        c_out = _attn_c(cqt, ck, cvt, batch=batch, seq=seq, bq=C_QUERY_BLOCK, bk=bk)
        x2 = _out_proj(x2, a_out, b_out, c_out, w_out[l].astype(BF16), row(g_post_mix[l]), bm=ROW_BLOCK)
        x2 = _mlp(x2, row(g_pre_mlp[l]), w_mlp_in[l].astype(BF16), w_mlp_out[l].astype(BF16),
                  row(g_post_mlp[l]), bm=ROW_BLOCK, fc=1024)
    return x2.reshape(batch, seq, D_MODEL)
```

```python
import functools
import math

import jax
import jax.numpy as jnp
from jax import lax
from jax.experimental import pallas as pl
from jax.experimental.pallas import tpu as pltpu

D_MODEL = 1024
HEAD_DIM = 64
LANES = 128
A_HEADS = 4
B_HEADS = 4
B_KV = 2
C_HEADS = 4
C_KV = 2
WINDOW = 128
GRID_W = 64
ROPE_THETA = 10000.0
D_FF = 4 * D_MODEL
EPS = 1e-6
N_ALIBI = A_HEADS + B_HEADS
ALIBI_SLOPES = tuple(2.0 ** (-8.0 * (i + 1) / N_ALIBI) for i in range(N_ALIBI))
A_SLOPES = ALIBI_SLOPES[0::2]
B_SLOPES = ALIBI_SLOPES[1::2]
SCALE = HEAD_DIM ** -0.5
LOG2E = math.log2(math.e)
SCALE_LOG2E = SCALE * LOG2E
NEG_BIG = -1e30

AQ0, AK0, AV0 = 0, 512, 1024
BQ0, BK0, BV0 = 1536, 1792, 1920
CQ0, CK0, CV0 = 2048, 2304, 2432

VMEM_LIMIT = 56 * 1024 * 1024

ROW_BLOCK = 512
KEY_CHUNK = 512
A_QUERY_BLOCK = 256
C_QUERY_BLOCK = 256
B_QUERY_BLOCK = 256
SUB_KEYS = 128
SUM_ROWS = 16
A_VT_ROWS = LANES + SUM_ROWS
C_VT_ROWS = HEAD_DIM + SUM_ROWS
A_HEADS_PER_STEP = 2
A_SCORE_SLOTS = 3
C_SCORE_SLOTS = 3

F32 = jnp.float32
BF16 = jnp.bfloat16
NT_DIMS = (((1,), (1,)), ((), ()))


def _rms(x, g):
    ms = jnp.mean(x * x, axis=-1, keepdims=True)
    return x * lax.rsqrt(ms + EPS) * g


def _proj_kernel(x_ref, g_ref, wr_ref, wt_ref, cos_ref, s1_ref, s2_ref, gk_ref, grp_ref,
                 cost_ref, s1t_ref, s2t_ref, gqt_ref, kaug_ref, qaugt_ref,
                 aqt_ref, ak_ref, avt_ref, bqt_ref, bk_ref, bvt_ref, cqt_ref, ck_ref, cvt_ref):
    bm = x_ref.shape[0]
    h =_rms(x_ref[...], g_ref[...]).astype(BF16)
    low = lax.broadcasted_iota(jnp.int32, (bm, LANES), 1) < HEAD_DIM

    def seg(c0, width):
        full = jnp.dot(h, wr_ref[:, c0:c0 + width], preferred_element_type=F32)
        return [full[:, j * LANES:(j + 1) * LANES] for j in range(width // LANES)]

    def seg_t(r0, rows):
        return lax.dot_general(wt_ref[r0:r0 + rows, :], h, NT_DIMS, preferred_element_type=F32)

    def put_slots(ref, j, blk, pad=0.0):
        ref[:, (2 * j) * LANES:(2 * j + 1) * LANES] = jnp.where(low, blk, pad).astype(BF16)
        ref[:, (2 * j + 1) * LANES:(2 * j + 2) * LANES] = jnp.where(
            low, pltpu.roll(blk, HEAD_DIM, 1), pad).astype(BF16)

    def put_slots_t(ref, xt, pads=None):
        zeros = jnp.zeros((HEAD_DIM, bm), BF16)
        for s in range(xt.shape[0] // HEAD_DIM):
            ref[s * LANES:s * LANES + HEAD_DIM, :] = xt[s * HEAD_DIM:(s + 1) * HEAD_DIM].astype(BF16)
            ref[s * LANES + HEAD_DIM:(s + 1) * LANES, :] = zeros if pads is None else pads[s]

    def put_chunks_t(ref, xt, width):
        chunk = ref.shape[-1]
        ones_row = jnp.where(lax.broadcasted_iota(jnp.int32, (SUM_ROWS, chunk), 0) == 0,
                             1.0, 0.0).astype(BF16)
        ext = width + SUM_ROWS
        for c in range(bm // chunk):
            for hh in range(xt.shape[0] // width):
                ref[c, hh * ext:hh * ext + width, :] = xt[hh * width:(hh + 1) * width,
                                                          c * chunk:(c + 1) * chunk].astype(BF16)
                ref[c, hh * ext + width:(hh + 1) * ext, :] = ones_row

    def norm_rope(blk, gain):
        ms = jnp.dot(blk * blk, grp_ref[...], preferred_element_type=F32,
                     precision=lax.Precision.HIGHEST)
        y = blk * lax.rsqrt(ms + EPS) * gain
        return (y * cos_ref[...] + pltpu.roll(y, LANES - 16, 1) * s1_ref[...]
                + pltpu.roll(y, 16, 1) * s2_ref[...])

    def norm_rope_t(xt):
        ms = jnp.mean(xt * xt, axis=0, keepdims=True)
        y = xt * lax.rsqrt(ms + EPS) * jnp.tile(gqt_ref[...], (1, bm // LANES))
        up = jnp.concatenate([y[16:], y[:16]], axis=0)
        dn = jnp.concatenate([y[HEAD_DIM - 16:], y[:HEAD_DIM - 16]], axis=0)
        return y * cost_ref[...] + up * s1t_ref[...] + dn * s2t_ref[...]

    for j, blk in enumerate(seg(0, 512)):
        put_slots(ak_ref, j, blk, kaug_ref[j])
    kb, kc = seg(512, 256)
    put_slots(bk_ref, 0, kb)
    put_slots(ck_ref, 0, norm_rope(kc, gk_ref[...]))

    put_slots_t(bqt_ref, seg_t(1408, 256) * SCALE_LOG2E)
    put_chunks_t(bvt_ref, seg_t(1664, 128), HEAD_DIM)
    put_slots_t(aqt_ref, seg_t(0, 512) * SCALE_LOG2E,
                [qaugt_ref[s // 2].astype(BF16) for s in range(2 * A_HEADS)])
    put_chunks_t(avt_ref, seg_t(512, 512), LANES)
    cqt = seg_t(1024, 256)
    cqt = jnp.concatenate([norm_rope_t(cqt[hh * HEAD_DIM:(hh + 1) * HEAD_DIM])
                           for hh in range(C_HEADS)], axis=0)
    put_slots_t(cqt_ref, cqt * SCALE_LOG2E)
    put_chunks_t(cvt_ref, seg_t(1280, 128), HEAD_DIM)


def _proj(x2, g, wr, wt, tabs, tabs_t, gk, gqt, grp, kaug, qaugt, *, batch, seq, bm, bk):
    n = x2.shape[0]
    nsb = seq // bm
    cpb = bm // bk
    row = lambda i: (i, 0)
    fixed = lambda i: (0, 0)
    tab = lambda i: (i % nsb, 0)
    tab_t = lambda i: (0, i % nsb)
    tok_t = lambda i: (i // nsb, 0, i % nsb)
    chunk_t = lambda i: (i // nsb, i % nsb, 0, 0)
    out_specs = [
        pl.BlockSpec((None, 8 * LANES, bm), tok_t),
        pl.BlockSpec((bm, 8 * LANES), row),
        pl.BlockSpec((None, cpb, A_HEADS * A_VT_ROWS, bk), chunk_t),
        pl.BlockSpec((None, 4 * LANES, bm), tok_t),
        pl.BlockSpec((bm, 2 * LANES), row),
        pl.BlockSpec((None, bm // WINDOW, B_KV * C_VT_ROWS, WINDOW), chunk_t),
        pl.BlockSpec((None, 4 * LANES, bm), tok_t),
        pl.BlockSpec((bm, 2 * LANES), row),
        pl.BlockSpec((None, cpb, C_KV * C_VT_ROWS, bk), chunk_t),
    ]
    out_shape = [
        jax.ShapeDtypeStruct((batch, 8 * LANES, seq), BF16),
        jax.ShapeDtypeStruct((n, 8 * LANES), BF16),
        jax.ShapeDtypeStruct((batch, seq // bk, A_HEADS * A_VT_ROWS, bk), BF16),
        jax.ShapeDtypeStruct((batch, 4 * LANES, seq), BF16),
        jax.ShapeDtypeStruct((n, 2 * LANES), BF16),
        jax.ShapeDtypeStruct((batch, seq // WINDOW, B_KV * C_VT_ROWS, WINDOW), BF16),
        jax.ShapeDtypeStruct((batch, 4 * LANES, seq), BF16),
        jax.ShapeDtypeStruct((n, 2 * LANES), BF16),
        jax.ShapeDtypeStruct((batch, seq // bk, C_KV * C_VT_ROWS, bk), BF16),
    ]
    return pl.pallas_call(
        _proj_kernel,
        grid=(n // bm,),
        in_specs=[
            pl.BlockSpec((bm, D_MODEL), row),
            pl.BlockSpec((1, D_MODEL), fixed),
            pl.BlockSpec(wr.shape, fixed, pipeline_mode=pl.Buffered(1)),
            pl.BlockSpec(wt.shape, fixed, pipeline_mode=pl.Buffered(1)),
            pl.BlockSpec((bm, LANES), tab),
            pl.BlockSpec((bm, LANES), tab),
            pl.BlockSpec((bm, LANES), tab),
            pl.BlockSpec((1, LANES), fixed),
            pl.BlockSpec((LANES, LANES), fixed),
            pl.BlockSpec((HEAD_DIM, bm), tab_t),
            pl.BlockSpec((HEAD_DIM, bm), tab_t),
            pl.BlockSpec((HEAD_DIM, bm), tab_t),
            pl.BlockSpec((HEAD_DIM, LANES), fixed),
            pl.BlockSpec((A_HEADS, bm, LANES), lambda i: (0, i % nsb, 0)),
            pl.BlockSpec((A_HEADS, HEAD_DIM, bm), lambda i: (0, 0, i % nsb)),
        ],
        out_specs=out_specs,
        out_shape=out_shape,
        name="in_proj",
        compiler_params=pltpu.CompilerParams(
            dimension_semantics=("arbitrary",), vmem_limit_bytes=VMEM_LIMIT),
    )(x2, g, wr, wt, *tabs, gk, grp, *tabs_t, gqt, kaug, qaugt)


def _fold8(x, op):
    out = x[0:8]
    for r in range(8, x.shape[0], 8):
        out = op(out, x[r:r + 8])
    return out


def _attn_a_kernel(c_ref, qt_ref, k_ref, vt_ref, lam_ref, g_ref,
                   o_ref, acc_ref, s_ref, p_ref, *, lam_init, bq, bk, seq):
    i = pl.program_id(2)
    nk = seq // bk
    nstream = acc_ref.shape[0]
    ndiag = max(1, bq // bk)
    jd = (i * bq) // bk
    aug = lax.broadcasted_iota(jnp.int32, (LANES, bq), 0) >= HEAD_DIM
    qts = [qt_ref[t * LANES:(t + 1) * LANES, :] for t in range(nstream)]
    acc_ref[...] = jnp.zeros(acc_ref.shape, F32)

    def issue_scores(jj):
        j = jd + jj
        j = jnp.where(j >= nk, j - nk, j)
        k0 = pl.multiple_of(j * bk, bk)
        maxes = []
        for t in range(nstream):
            if jj < ndiag:
                qv = jnp.where(aug, jnp.zeros_like(qts[t]), qts[t])
            else:
                qv = jnp.where(jnp.logical_and(aug, j > jd), -qts[t], qts[t])
            mx = None
            for r in range(bk // SUB_KEYS):
                r0 = r * SUB_KEYS
                s = jnp.dot(k_ref[pl.ds(k0 + r0, SUB_KEYS), t * LANES:(t + 1) * LANES], qv,
                            preferred_element_type=F32)
                if jj < ndiag:
                    rowcol = (lax.broadcasted_iota(jnp.int32, (SUB_KEYS, bq), 0)
                              - lax.broadcasted_iota(jnp.int32, (SUB_KEYS, bq), 1)
                              + (j * bk + r0 - i * bq)).astype(F32)
                    s = s - c_ref[t // 2, :, :1] * jnp.abs(rowcol)
                s_ref[jj % nslot, t, r0:r0 + SUB_KEYS, :] = s
                pm = _fold8(s, jnp.maximum)
                mx = pm if mx is None else jnp.maximum(mx, pm)
            maxes.append(jnp.max(mx, axis=0, keepdims=True))
        return j, maxes

    nslot = s_ref.shape[0]
    ahead = nslot - 1
    m = [jnp.full((1, bq), NEG_BIG, F32)] * nstream
    pending = [issue_scores(jj) for jj in range(min(ahead, nk))]
    for jj in range(nk):
        if jj + ahead < nk:
            pending.append(issue_scores(jj + ahead))
        j, maxes = pending.pop(0)
        for t in range(nstream):
            m_new = jnp.maximum(m[t], maxes[t])
            alpha = jnp.exp2(m[t] - m_new)
            m[t] = m_new
            for r in range(bk // SUB_KEYS):
                r0 = r * SUB_KEYS
                p = jnp.exp2(s_ref[jj % nslot, t, r0:r0 + SUB_KEYS, :] - m_new)
                p_ref[jj % 2, t, r0:r0 + SUB_KEYS, :] = p.astype(BF16)
            acc_ref[t] = alpha * acc_ref[t] + jnp.dot(
                vt_ref[j, (t // 2) * A_VT_ROWS:(t // 2 + 1) * A_VT_ROWS, :], p_ref[jj % 2, t],
                preferred_element_type=F32)

    lam = (jnp.exp(jnp.sum(lam_ref[0:1] * lam_ref[1:2], axis=-1, keepdims=True))
           - jnp.exp(jnp.sum(lam_ref[2:3] * lam_ref[3:4], axis=-1, keepdims=True))
           + lam_init)

    def normalised(t):
        return acc_ref[t, 0:LANES, :] / acc_ref[t, LANES:LANES + 1, :]

    for hh in range(nstream // 2):
        o = (normalised(2 * hh) - lam * normalised(2 * hh + 1)).T
        o_ref[:, hh * LANES:(hh + 1) * LANES] = (
            _rms(o, g_ref[...]) * (1.0 - lam_init)).astype(BF16)


def _attn_a(ctab, aqt, ak, avt, lam_vecs, g, *, lam_init, batch, seq, bq, bk, hps, nslot):
    n = ak.shape[0]
    nq = seq // bq
    vec = lambda b, h, i: (0, 0)
    kern = functools.partial(_attn_a_kernel, lam_init=lam_init, bq=bq, bk=bk, seq=seq)
    return pl.pallas_call(
        kern,
        grid=(batch, A_HEADS // hps, nq),
        in_specs=[
            pl.BlockSpec((hps, 1, LANES), lambda b, h, i: (h, 0, 0)),
            pl.BlockSpec((None, hps * 2 * LANES, bq), lambda b, h, i: (b, h, i)),
            pl.BlockSpec((seq, hps * 2 * LANES), lambda b, h, i: (b, h)),
            pl.BlockSpec((None, seq // bk, hps * A_VT_ROWS, bk), lambda b, h, i: (b, 0, h, 0)),
            pl.BlockSpec((4, HEAD_DIM), vec),
            pl.BlockSpec((1, LANES), vec),
        ],
        out_specs=pl.BlockSpec((bq, hps * LANES), lambda b, h, i: (b * nq + i, h)),
        out_shape=jax.ShapeDtypeStruct((n, A_HEADS * LANES), BF16),
        name="attn_diff",
        scratch_shapes=[pltpu.VMEM((2 * hps, A_VT_ROWS, bq), F32),
                        pltpu.VMEM((nslot, 2 * hps, bk, bq), F32),
                        pltpu.VMEM((2, 2 * hps, bk, bq), BF16)],
        compiler_params=pltpu.CompilerParams(
            dimension_semantics=("arbitrary", "arbitrary", "arbitrary"),
            vmem_limit_bytes=VMEM_LIMIT),
    )(ctab, aqt, ak, avt, lam_vecs, g)


def _attn_b_kernel(c_ref, sink_ref, qt_ref, k_ref, vt_ref, o_ref, s_ref, p_ref, *, bq, seq):
    i = pl.program_id(1)
    win = bq + 2 * WINDOW
    ks = pl.multiple_of(jnp.clip(i * bq - WINDOW, 0, seq - win), WINDOW)
    c0 = ks // WINDOW
    qts = [qt_ref[t * LANES:(t + 1) * LANES, :] for t in range(B_HEADS)]

    mx = [None] * B_HEADS
    for r in range(win // SUB_KEYS):
        r0 = r * SUB_KEYS
        dist = jnp.abs(lax.broadcasted_iota(jnp.int32, (SUB_KEYS, bq), 0)
                       - lax.broadcasted_iota(jnp.int32, (SUB_KEYS, bq), 1)
                       + (ks + r0 - i * bq))
        in_band = dist <= WINDOW
        dist = dist.astype(F32)
        for t in range(B_HEADS):
            g = t // (B_HEADS // B_KV)
            s = jnp.dot(k_ref[pl.ds(ks + r0, SUB_KEYS), g * LANES:(g + 1) * LANES], qts[t],
                        preferred_element_type=F32)
            s = jnp.where(in_band, s - c_ref[t, :, :1] * dist, NEG_BIG)
            s_ref[t, r0:r0 + SUB_KEYS, :] = s
            pm = _fold8(s, jnp.maximum)
            mx[t] = pm if mx[t] is None else jnp.maximum(mx[t], pm)

    outs = []
    for t in range(B_HEADS):
        g = t // (B_HEADS // B_KV)
        sink = sink_ref[t, :, :1]
        m = jnp.maximum(jnp.max(mx[t], axis=0, keepdims=True), sink)
        for r in range(win // SUB_KEYS):
            r0 = r * SUB_KEYS
            p_ref[t, r0:r0 + SUB_KEYS, :] = jnp.exp2(s_ref[t, r0:r0 + SUB_KEYS, :] - m).astype(BF16)
        vt = jnp.concatenate([vt_ref[c0 + u, g * C_VT_ROWS:(g + 1) * C_VT_ROWS, :]
                              for u in range(win // WINDOW)], axis=1)
        acc = jnp.dot(vt, p_ref[t], preferred_element_type=F32)
        denom = acc[HEAD_DIM:HEAD_DIM + 1, :] + jnp.exp2(sink - m)
        outs.append(acc[0:HEAD_DIM, :] / denom)
    o_ref[...] = jnp.concatenate(outs, axis=0).T.astype(BF16)


def _attn_b(ctab, sinks, bqt, bk, bvt, *, batch, seq, bq):
    n = bk.shape[0]
    nq = seq // bq
    win = bq + 2 * WINDOW
    tab = lambda b, i: (0, 0, 0)
    return pl.pallas_call(
        functools.partial(_attn_b_kernel, bq=bq, seq=seq),
        grid=(batch, nq),
        in_specs=[
            pl.BlockSpec((B_HEADS, 1, LANES), tab),
            pl.BlockSpec((B_HEADS, 1, LANES), tab),
            pl.BlockSpec((None, B_HEADS * LANES, bq), lambda b, i: (b, 0, i)),
            pl.BlockSpec((seq, B_KV * LANES), lambda b, i: (b, 0)),
            pl.BlockSpec((None, seq // WINDOW, B_KV * C_VT_ROWS, WINDOW), lambda b, i: (b, 0, 0, 0)),
        ],
        out_specs=pl.BlockSpec((bq, B_HEADS * HEAD_DIM), lambda b, i: (b * nq + i, 0)),
        out_shape=jax.ShapeDtypeStruct((n, B_HEADS * HEAD_DIM), BF16),
        name="attn_window",
        scratch_shapes=[pltpu.VMEM((B_HEADS, win, bq), F32),
                        pltpu.VMEM((B_HEADS, win, bq), BF16)],
        compiler_params=pltpu.CompilerParams(
            dimension_semantics=("arbitrary", "arbitrary"),
            vmem_limit_bytes=VMEM_LIMIT),
    )(ctab, sinks, bqt, bk, bvt)


def _attn_c_kernel(qt_ref, k_ref, vt_ref, o_ref, acc_ref, s_ref, p_ref, *, bq, bk, seq):
    nk = seq // bk
    nstream = acc_ref.shape[0]
    qts = [qt_ref[t * LANES:(t + 1) * LANES, :] for t in range(nstream)]
    acc_ref[...] = jnp.zeros(acc_ref.shape, F32)

    def issue_scores(j):
        maxes = []
        for t in range(nstream):
            mx = None
            for r in range(bk // SUB_KEYS):
                r0 = j * bk + r * SUB_KEYS
                s = jnp.dot(k_ref[r0:r0 + SUB_KEYS, :], qts[t], preferred_element_type=F32)
                s_ref[j % nslot, t, r * SUB_KEYS:(r + 1) * SUB_KEYS, :] = s
                pm = _fold8(s, jnp.maximum)
                mx = pm if mx is None else jnp.maximum(mx, pm)
            maxes.append(jnp.max(mx, axis=0, keepdims=True))
        return maxes

    nslot = s_ref.shape[0]
    ahead = nslot - 1
    m = [jnp.full((1, bq), NEG_BIG, F32)] * nstream
    pending = [issue_scores(j) for j in range(min(ahead, nk))]
    for j in range(nk):
        if j + ahead < nk:
            pending.append(issue_scores(j + ahead))
        maxes = pending.pop(0)
        for t in range(nstream):
            m_new = jnp.maximum(m[t], maxes[t])
            alpha = jnp.exp2(m[t] - m_new)
            m[t] = m_new
            for r in range(bk // SUB_KEYS):
                r0 = r * SUB_KEYS
                p = jnp.exp2(s_ref[j % nslot, t, r0:r0 + SUB_KEYS, :] - m_new)
                p_ref[j % 2, t, r0:r0 + SUB_KEYS, :] = p.astype(BF16)
            acc_ref[t] = alpha * acc_ref[t] + jnp.dot(
                vt_ref[j], p_ref[j % 2, t], preferred_element_type=F32)
    ot = jnp.concatenate([acc_ref[t, 0:HEAD_DIM, :] / acc_ref[t, HEAD_DIM:HEAD_DIM + 1, :]
                          for t in range(nstream)], axis=0)
    o_ref[...] = ot.T.astype(BF16)


def _attn_c(cqt, ck, cvt, *, batch, seq, bq, bk):
    n = ck.shape[0]
    nq = seq // bq
    return pl.pallas_call(
        functools.partial(_attn_c_kernel, bq=bq, bk=bk, seq=seq),
        grid=(batch, C_KV, nq),
        in_specs=[
            pl.BlockSpec((None, 2 * LANES, bq), lambda b, g, i: (b, g, i)),
            pl.BlockSpec((seq, LANES), lambda b, g, i: (b, g)),
            pl.BlockSpec((None, seq // bk, C_VT_ROWS, bk), lambda b, g, i: (b, 0, g, 0)),
        ],
        out_specs=pl.BlockSpec((bq, LANES), lambda b, g, i: (b * nq + i, g)),
        out_shape=jax.ShapeDtypeStruct((n, C_HEADS * HEAD_DIM), BF16),
        name="attn_global",
        scratch_shapes=[pltpu.VMEM((2, C_VT_ROWS, bq), F32),
                        pltpu.VMEM((C_SCORE_SLOTS, 2, bk, bq), F32),
                        pltpu.VMEM((2, 2, bk, bq), BF16)],
        compiler_params=pltpu.CompilerParams(
            dimension_semantics=("arbitrary", "arbitrary", "arbitrary"),
            vmem_limit_bytes=VMEM_LIMIT),
    )(cqt, ck, cvt)


def _out_kernel(x_ref, a_ref, b_ref, c_ref, w_ref, g_ref, o_ref):
    mix = jnp.dot(a_ref[...], w_ref[0:512, :], preferred_element_type=F32)
    mix += jnp.dot(b_ref[...], w_ref[512:768, :], preferred_element_type=F32)
    mix += jnp.dot(c_ref[...], w_ref[768:1024, :], preferred_element_type=F32)
    o_ref[...] = x_ref[...] + _rms(mix, g_ref[...])


def _out_proj(x2, a, b, c, w, g, *, bm):
    n = x2.shape[0]
    row = lambda i: (i, 0)
    fixed = lambda i: (0, 0)
    return pl.pallas_call(
        _out_kernel,
        grid=(n // bm,),
        in_specs=[
            pl.BlockSpec((bm, D_MODEL), row),
            pl.BlockSpec((bm, 512), row),
            pl.BlockSpec((bm, 256), row),
            pl.BlockSpec((bm, 256), row),
            pl.BlockSpec((D_MODEL, D_MODEL), fixed, pipeline_mode=pl.Buffered(1)),
            pl.BlockSpec((1, D_MODEL), fixed),
        ],
        out_specs=pl.BlockSpec((bm, D_MODEL), row),
        out_shape=jax.ShapeDtypeStruct((n, D_MODEL), F32),
        name="out_proj",
        compiler_params=pltpu.CompilerParams(
            dimension_semantics=("arbitrary",), vmem_limit_bytes=VMEM_LIMIT),
    )(x2, a, b, c, w, g)


def _mlp_kernel(x_ref, g1_ref, w1_ref, w2_ref, g2_ref, o_ref, *, fc):
    x = x_ref[...]
    h = _rms(x, g1_ref[...]).astype(BF16)
    y = jnp.zeros(x.shape, F32)
    for f in range(D_FF // fc):
        t = jnp.dot(h, w1_ref[:, f * fc:(f + 1) * fc], preferred_element_type=F32)
        t = jnp.square(jnp.maximum(t, 0.0)).astype(BF16)
        y += jnp.dot(t, w2_ref[f * fc:(f + 1) * fc, :], preferred_element_type=F32)
    o_ref[...] = x + _rms(y, g2_ref[...])


def _mlp(x2, g1, w1, w2, g2, *, bm, fc):
    n = x2.shape[0]
    row = lambda i: (i, 0)
    fixed = lambda i: (0, 0)
    return pl.pallas_call(
        functools.partial(_mlp_kernel, fc=fc),
        grid=(n // bm,),
        in_specs=[
            pl.BlockSpec((bm, D_MODEL), row),
            pl.BlockSpec((1, D_MODEL), fixed),
            pl.BlockSpec((D_MODEL, D_FF), fixed, pipeline_mode=pl.Buffered(1)),
            pl.BlockSpec((D_FF, D_MODEL), fixed, pipeline_mode=pl.Buffered(1)),
            pl.BlockSpec((1, D_MODEL), fixed),
        ],
        out_specs=pl.BlockSpec((bm, D_MODEL), row),
        out_shape=jax.ShapeDtypeStruct((n, D_MODEL), F32),
        name="mlp",
        compiler_params=pltpu.CompilerParams(
            dimension_semantics=("arbitrary",), vmem_limit_bytes=VMEM_LIMIT),
    )(x2, g1, w1, w2, g2)


def _rope_tables(seq):
    pos = jnp.arange(seq)
    row = (pos // GRID_W).astype(F32)
    col = (pos % GRID_W).astype(F32)
    half = HEAD_DIM // 2
    freqs = ROPE_THETA ** (-jnp.arange(0, half, 2, dtype=F32) / half)
    d = jnp.arange(HEAD_DIM)
    ang = jnp.where((d < half)[None, :], row[:, None], col[:, None]) * freqs[d % 16][None, :]
    first = ((d % half) < 16)[None, :]
    cos, sin = jnp.cos(ang), jnp.sin(ang)
    return cos, jnp.where(first, -sin, 0.0), jnp.where(first, 0.0, sin)


def _alibi_tables(seq):
    pos = jnp.arange(seq)
    parts = [((pos // 64) * 64).astype(F32), (pos % 64).astype(F32)]
    pos4 = [jnp.broadcast_to(p[None, :], (A_HEADS, seq)) for p in parts * 2]
    c = jnp.asarray(A_SLOPES, F32) * LOG2E
    c_hi = c.astype(BF16).astype(F32)
    c_lo = (c - c_hi).astype(BF16).astype(F32)
    c4 = [jnp.broadcast_to(v[:, None], (A_HEADS, seq)) for v in (c_hi, c_hi, c_lo, c_lo)]
    kaug = jnp.zeros((A_HEADS, seq, LANES), F32).at[:, :, HEAD_DIM:HEAD_DIM + 8].set(
        jnp.stack([-v for v in c4] + pos4, axis=-1))
    qaugt = jnp.zeros((A_HEADS, HEAD_DIM, seq), F32).at[:, 0:8, :].set(
        jnp.stack(pos4 + c4, axis=1))
    c_tab = jnp.broadcast_to(c[:, None, None], (A_HEADS, 1, LANES))
    return kaug, qaugt, c_tab


def kernel(x, w_in, w_out, g_pre_mix, g_post_mix, lam_q1, lam_k1, lam_q2, lam_k2,
           diff_subln_g, sink_logits, c_q_norm, c_k_norm, g_pre_mlp, g_post_mlp,
           w_mlp_in, w_mlp_out):
    batch, seq, _ = x.shape
    depth = w_in.shape[0]
    n = batch * seq
    x2 = x.reshape(n, D_MODEL)
    tabs64 = _rope_tables(seq)
    tabs = tuple(jnp.tile(t, (1, 2)) for t in tabs64)
    tabs_t = tuple(t.T for t in tabs64)
    lane = jnp.arange(LANES)
    grp = jnp.where((lane[:, None] // HEAD_DIM) == (lane[None, :] // HEAD_DIM),
                    1.0 / HEAD_DIM, 0.0).astype(F32)
    kaug, qaugt, a_ctab = _alibi_tables(seq)
    b_ctab = jnp.broadcast_to((jnp.asarray(B_SLOPES, F32) * LOG2E)[:, None, None],
                              (B_HEADS, 1, LANES))
    row = lambda v: v.reshape(1, -1).astype(F32)
    bk = KEY_CHUNK

    for l in range(depth):
        lam_init = 0.8 - 0.6 * math.exp(-0.3 * l)
        w = w_in[l].astype(BF16)
        wr = jnp.concatenate([w[:, AK0:AV0], w[:, BK0:BV0], w[:, CK0:CV0]], axis=1)
        wt = jnp.concatenate([w[:, AQ0:AK0], w[:, AV0:BQ0], w[:, CQ0:CK0], w[:, CV0:],
                              w[:, BQ0:BK0], w[:, BV0:CQ0]], axis=1).T
        gqt = jnp.broadcast_to(c_q_norm[l].astype(F32)[:, None], (HEAD_DIM, LANES))
        aqt, ak, avt, bqt, bkk, bvt, cqt, ck, cvt = _proj(
            x2, row(g_pre_mix[l]), wr, wt, tabs, tabs_t, row(jnp.tile(c_k_norm[l], 2)), gqt, grp,
            kaug, qaugt, batch=batch, seq=seq, bm=ROW_BLOCK, bk=bk)
        lam_vecs = jnp.stack([lam_q1[l], lam_k1[l], lam_q2[l], lam_k2[l]]).astype(F32)
        a_out = _attn_a(a_ctab, aqt, ak, avt, lam_vecs, row(diff_subln_g[l]), lam_init=lam_init,
                        batch=batch, seq=seq, bq=A_QUERY_BLOCK, bk=bk,
                        hps=A_HEADS_PER_STEP, nslot=A_SCORE_SLOTS)
        sinks = jnp.broadcast_to((sink_logits[l].astype(F32) * LOG2E)[:, None, None],
                                 (B_HEADS, 1, LANES))
        b_out = _attn_b(b_ctab, sinks, bqt, bkk, bvt, batch=batch, seq=seq, bq=B_QUERY_BLOCK)
        c_out = _attn_c(cqt, ck, cvt, batch=batch, seq=seq, bq=C_QUERY_BLOCK, bk=bk)
        x2 = _out_proj(x2, a_out, b_out, c_out, w_out[l].astype(BF16), row(g_post_mix[l]), bm=ROW_BLOCK)
        x2 = _mlp(x2, row(g_pre_mlp[l]), w_mlp_in[l].astype(BF16), w_mlp_out[l].astype(BF16),
                  row(g_post_mlp[l]), bm=ROW_BLOCK, fc=1024)
    return x2.reshape(batch, seq, D_MODEL)
```

```python
import functools
import math

import jax
import jax.numpy as jnp
from jax import lax
from jax.experimental import pallas as pl
from jax.experimental.pallas import tpu as pltpu

D_MODEL = 1024
HEAD_DIM = 64
LANES = 128
A_HEADS = 4
B_HEADS = 4
B_KV = 2
C_HEADS = 4
C_KV = 2
WINDOW = 128
GRID_W = 64
ROPE_THETA = 10000.0
D_FF = 4 * D_MODEL
EPS = 1e-6
N_ALIBI = A_HEADS + B_HEADS
ALIBI_SLOPES = tuple(2.0 ** (-8.0 * (i + 1) / N_ALIBI) for i in range(N_ALIBI))
A_SLOPES = ALIBI_SLOPES[0::2]
B_SLOPES = ALIBI_SLOPES[1::2]
SCALE = HEAD_DIM ** -0.5
LOG2E = math.log2(math.e)
SCALE_LOG2E = SCALE * LOG2E
NEG_BIG = -1e30

AQ0, AK0, AV0 = 0, 512, 1024
BQ0, BK0, BV0 = 1536, 1792, 1920
CQ0, CK0, CV0 = 2048, 2304, 2432

VMEM_LIMIT = 56 * 1024 * 1024

ROW_BLOCK = 512
FF_CHUNK = 1024
KEY_CHUNK = 512
A_QUERY_BLOCK = 256
C_QUERY_BLOCK = 256
B_QUERY_BLOCK = 256
SUB_KEYS = 128
SUM_ROWS = 16
A_VT_ROWS = LANES + SUM_ROWS
C_VT_ROWS = HEAD_DIM + SUM_ROWS
A_HEADS_PER_STEP = 2
A_SCORE_SLOTS = 3
C_SCORE_SLOTS = 3

F32 = jnp.float32
BF16 = jnp.bfloat16
NT_DIMS = (((1,), (1,)), ((), ()))


def _rms(x, g):
    ms = jnp.mean(x * x, axis=-1, keepdims=True)
    return x * lax.rsqrt(ms + EPS) * g


def _proj_kernel(x_ref, g_ref, wr_ref, wt_ref, cos_ref, s1_ref, s2_ref, gk_ref, grp_ref,
                 cost_ref, s1t_ref, s2t_ref, gqt_ref, kaug_ref, qaugt_ref,
                 aqt_ref, ak_ref, avt_ref, bqt_ref, bk_ref, bvt_ref, cqt_ref, ck_ref, cvt_ref):
    bm = x_ref.shape[0]
    h =_rms(x_ref[...], g_ref[...]).astype(BF16)
    low = lax.broadcasted_iota(jnp.int32, (bm, LANES), 1) < HEAD_DIM

    def seg(c0, width):
        full = jnp.dot(h, wr_ref[:, c0:c0 + width], preferred_element_type=F32)
        return [full[:, j * LANES:(j + 1) * LANES] for j in range(width // LANES)]

    def seg_t(r0, rows):
        return lax.dot_general(wt_ref[r0:r0 + rows, :], h, NT_DIMS, preferred_element_type=F32)

    def put_slots(ref, j, blk, pad=0.0):
        ref[:, (2 * j) * LANES:(2 * j + 1) * LANES] = jnp.where(low, blk, pad).astype(BF16)
        ref[:, (2 * j + 1) * LANES:(2 * j + 2) * LANES] = jnp.where(
            low, pltpu.roll(blk, HEAD_DIM, 1), pad).astype(BF16)

    def put_slots_t(ref, xt, pads=None):
        zeros = jnp.zeros((HEAD_DIM, bm), BF16)
        for s in range(xt.shape[0] // HEAD_DIM):
            ref[s * LANES:s * LANES + HEAD_DIM, :] = xt[s * HEAD_DIM:(s + 1) * HEAD_DIM].astype(BF16)
            ref[s * LANES + HEAD_DIM:(s + 1) * LANES, :] = zeros if pads is None else pads[s]

    def put_chunks_t(ref, xt, width):
        chunk = ref.shape[-1]
        ones_row = jnp.where(lax.broadcasted_iota(jnp.int32, (SUM_ROWS, chunk), 0) == 0,
                             1.0, 0.0).astype(BF16)
        ext = width + SUM_ROWS
        for c in range(bm // chunk):
            for hh in range(xt.shape[0] // width):
                ref[c, hh * ext:hh * ext + width, :] = xt[hh * width:(hh + 1) * width,
                                                          c * chunk:(c + 1) * chunk].astype(BF16)
                ref[c, hh * ext + width:(hh + 1) * ext, :] = ones_row

    def norm_rope(blk, gain):
        ms = jnp.dot(blk * blk, grp_ref[...], preferred_element_type=F32,
                     precision=lax.Precision.HIGHEST)
        y = blk * lax.rsqrt(ms + EPS) * gain
        return (y * cos_ref[...] + pltpu.roll(y, LANES - 16, 1) * s1_ref[...]
                + pltpu.roll(y, 16, 1) * s2_ref[...])

    def norm_rope_t(xt):
        ms = jnp.mean(xt * xt, axis=0, keepdims=True)
        y = xt * lax.rsqrt(ms + EPS) * jnp.tile(gqt_ref[...], (1, bm // LANES))
        up = jnp.concatenate([y[16:], y[:16]], axis=0)
        dn = jnp.concatenate([y[HEAD_DIM - 16:], y[:HEAD_DIM - 16]], axis=0)
        return y * cost_ref[...] + up * s1t_ref[...] + dn * s2t_ref[...]

    for j, blk in enumerate(seg(0, 512)):
        put_slots(ak_ref, j, blk, kaug_ref[j])
    kb, kc = seg(512, 256)
    put_slots(bk_ref, 0, kb)
    put_slots(ck_ref, 0, norm_rope(kc, gk_ref[...]))

    put_slots_t(bqt_ref, seg_t(1408, 256) * SCALE_LOG2E)
    put_chunks_t(bvt_ref, seg_t(1664, 128), HEAD_DIM)
    put_slots_t(aqt_ref, seg_t(0, 512) * SCALE_LOG2E,
                [qaugt_ref[s // 2].astype(BF16) for s in range(2 * A_HEADS)])
    put_chunks_t(avt_ref, seg_t(512, 512), LANES)
    cqt = seg_t(1024, 256)
    cqt = jnp.concatenate([norm_rope_t(cqt[hh * HEAD_DIM:(hh + 1) * HEAD_DIM])
                           for hh in range(C_HEADS)], axis=0)
    put_slots_t(cqt_ref, cqt * SCALE_LOG2E)
    put_chunks_t(cvt_ref, seg_t(1280, 128), HEAD_DIM)


def _proj(x2, g, wr, wt, tabs, tabs_t, gk, gqt, grp, kaug, qaugt, *, batch, seq, bm, bk):
    n = x2.shape[0]
    nsb = seq // bm
    cpb = bm // bk
    row = lambda i: (i, 0)
    fixed = lambda i: (0, 0)
    tab = lambda i: (i % nsb, 0)
    tab_t = lambda i: (0, i % nsb)
    tok_t = lambda i: (i // nsb, 0, i % nsb)
    chunk_t = lambda i: (i // nsb, i % nsb, 0, 0)
    out_specs = [
        pl.BlockSpec((None, 8 * LANES, bm), tok_t),
        pl.BlockSpec((bm, 8 * LANES), row),
        pl.BlockSpec((None, cpb, A_HEADS * A_VT_ROWS, bk), chunk_t),
        pl.BlockSpec((None, 4 * LANES, bm), tok_t),
        pl.BlockSpec((bm, 2 * LANES), row),
        pl.BlockSpec((None, bm // WINDOW, B_KV * C_VT_ROWS, WINDOW), chunk_t),
        pl.BlockSpec((None, 4 * LANES, bm), tok_t),
        pl.BlockSpec((bm, 2 * LANES), row),
        pl.BlockSpec((None, cpb, C_KV * C_VT_ROWS, bk), chunk_t),
    ]
    out_shape = [
        jax.ShapeDtypeStruct((batch, 8 * LANES, seq), BF16),
        jax.ShapeDtypeStruct((n, 8 * LANES), BF16),
        jax.ShapeDtypeStruct((batch, seq // bk, A_HEADS * A_VT_ROWS, bk), BF16),
        jax.ShapeDtypeStruct((batch, 4 * LANES, seq), BF16),
        jax.ShapeDtypeStruct((n, 2 * LANES), BF16),
        jax.ShapeDtypeStruct((batch, seq // WINDOW, B_KV * C_VT_ROWS, WINDOW), BF16),
        jax.ShapeDtypeStruct((batch, 4 * LANES, seq), BF16),
        jax.ShapeDtypeStruct((n, 2 * LANES), BF16),
        jax.ShapeDtypeStruct((batch, seq // bk, C_KV * C_VT_ROWS, bk), BF16),
    ]
    return pl.pallas_call(
        _proj_kernel,
        grid=(n // bm,),
        in_specs=[
            pl.BlockSpec((bm, D_MODEL), row),
            pl.BlockSpec((1, D_MODEL), fixed),
            pl.BlockSpec(wr.shape, fixed, pipeline_mode=pl.Buffered(1)),
            pl.BlockSpec(wt.shape, fixed, pipeline_mode=pl.Buffered(1)),
            pl.BlockSpec((bm, LANES), tab),
            pl.BlockSpec((bm, LANES), tab),
            pl.BlockSpec((bm, LANES), tab),
            pl.BlockSpec((1, LANES), fixed),
            pl.BlockSpec((LANES, LANES), fixed),
            pl.BlockSpec((HEAD_DIM, bm), tab_t),
            pl.BlockSpec((HEAD_DIM, bm), tab_t),
            pl.BlockSpec((HEAD_DIM, bm), tab_t),
            pl.BlockSpec((HEAD_DIM, LANES), fixed),
            pl.BlockSpec((A_HEADS, bm, LANES), lambda i: (0, i % nsb, 0)),
            pl.BlockSpec((A_HEADS, HEAD_DIM, bm), lambda i: (0, 0, i % nsb)),
        ],
        out_specs=out_specs,
        out_shape=out_shape,
        name="in_proj",
        compiler_params=pltpu.CompilerParams(
            dimension_semantics=("arbitrary",), vmem_limit_bytes=VMEM_LIMIT),
    )(x2, g, wr, wt, *tabs, gk, grp, *tabs_t, gqt, kaug, qaugt)


def _fold8(x, op):
    out = x[0:8]
    for r in range(8, x.shape[0], 8):
        out = op(out, x[r:r + 8])
    return out


def _attn_a_kernel(c_ref, qt_ref, k_ref, vt_ref, lam_ref, g_ref,
                   o_ref, acc_ref, s_ref, p_ref, *, lam_init, bq, bk, seq):
    i = pl.program_id(2)
    nk = seq // bk
    nstream = acc_ref.shape[0]
    ndiag = max(1, bq // bk)
    jd = (i * bq) // bk
    aug = lax.broadcasted_iota(jnp.int32, (LANES, bq), 0) >= HEAD_DIM
    qts = [qt_ref[t * LANES:(t + 1) * LANES, :] for t in range(nstream)]
    acc_ref[...] = jnp.zeros(acc_ref.shape, F32)

    def issue_scores(jj):
        j = jd + jj
        j = jnp.where(j >= nk, j - nk, j)
        k0 = pl.multiple_of(j * bk, bk)
        maxes = []
        for t in range(nstream):
            if jj < ndiag:
                qv = jnp.where(aug, jnp.zeros_like(qts[t]), qts[t])
            else:
                qv = jnp.where(jnp.logical_and(aug, j > jd), -qts[t], qts[t])
            mx = None
            for r in range(bk // SUB_KEYS):
                r0 = r * SUB_KEYS
                s = jnp.dot(k_ref[pl.ds(k0 + r0, SUB_KEYS), t * LANES:(t + 1) * LANES], qv,
                            preferred_element_type=F32)
                if jj < ndiag:
                    rowcol = (lax.broadcasted_iota(jnp.int32, (SUB_KEYS, bq), 0)
                              - lax.broadcasted_iota(jnp.int32, (SUB_KEYS, bq), 1)
                              + (j * bk + r0 - i * bq)).astype(F32)
                    s = s - c_ref[t // 2, :, :1] * jnp.abs(rowcol)
                s_ref[jj % nslot, t, r0:r0 + SUB_KEYS, :] = s
                pm = _fold8(s, jnp.maximum)
                mx = pm if mx is None else jnp.maximum(mx, pm)
            maxes.append(jnp.max(mx, axis=0, keepdims=True))
        return j, maxes

    nslot = s_ref.shape[0]
    ahead = nslot - 1
    m = [jnp.full((1, bq), NEG_BIG, F32)] * nstream
    pending = [issue_scores(jj) for jj in range(min(ahead, nk))]
    for jj in range(nk):
        if jj + ahead < nk:
            pending.append(issue_scores(jj + ahead))
        j, maxes = pending.pop(0)
        for t in range(nstream):
            m_new = jnp.maximum(m[t], maxes[t])
            alpha = jnp.exp2(m[t] - m_new)
            m[t] = m_new
            for r in range(bk // SUB_KEYS):
                r0 = r * SUB_KEYS
                p = jnp.exp2(s_ref[jj % nslot, t, r0:r0 + SUB_KEYS, :] - m_new)
                p_ref[jj % 2, t, r0:r0 + SUB_KEYS, :] = p.astype(BF16)
            acc_ref[t] = alpha * acc_ref[t] + jnp.dot(
                vt_ref[j, (t // 2) * A_VT_ROWS:(t // 2 + 1) * A_VT_ROWS, :], p_ref[jj % 2, t],
                preferred_element_type=F32)

    lam = (jnp.exp(jnp.sum(lam_ref[0:1] * lam_ref[1:2], axis=-1, keepdims=True))
           - jnp.exp(jnp.sum(lam_ref[2:3] * lam_ref[3:4], axis=-1, keepdims=True))
           + lam_init)

    def normalised(t):
        return acc_ref[t, 0:LANES, :] / acc_ref[t, LANES:LANES + 1, :]

    for hh in range(nstream // 2):
        o = (normalised(2 * hh) - lam * normalised(2 * hh + 1)).T
        o_ref[:, hh * LANES:(hh + 1) * LANES] = (
            _rms(o, g_ref[...]) * (1.0 - lam_init)).astype(BF16)


def _attn_a(ctab, aqt, ak, avt, lam_vecs, g, *, lam_init, batch, seq, bq, bk, hps, nslot):
    n = ak.shape[0]
    nq = seq // bq
    vec = lambda b, h, i: (0, 0)
    kern = functools.partial(_attn_a_kernel, lam_init=lam_init, bq=bq, bk=bk, seq=seq)
    return pl.pallas_call(
        kern,
        grid=(batch, A_HEADS // hps, nq),
        in_specs=[
            pl.BlockSpec((hps, 1, LANES), lambda b, h, i: (h, 0, 0)),
            pl.BlockSpec((None, hps * 2 * LANES, bq), lambda b, h, i: (b, h, i)),
            pl.BlockSpec((seq, hps * 2 * LANES), lambda b, h, i: (b, h)),
            pl.BlockSpec((None, seq // bk, hps * A_VT_ROWS, bk), lambda b, h, i: (b, 0, h, 0)),
            pl.BlockSpec((4, HEAD_DIM), vec),
            pl.BlockSpec((1, LANES), vec),
        ],
        out_specs=pl.BlockSpec((bq, hps * LANES), lambda b, h, i: (b * nq + i, h)),
        out_shape=jax.ShapeDtypeStruct((n, A_HEADS * LANES), BF16),
        name="attn_diff",
        scratch_shapes=[pltpu.VMEM((2 * hps, A_VT_ROWS, bq), F32),
                        pltpu.VMEM((nslot, 2 * hps, bk, bq), F32),
                        pltpu.VMEM((2, 2 * hps, bk, bq), BF16)],
        compiler_params=pltpu.CompilerParams(
            dimension_semantics=("arbitrary", "arbitrary", "arbitrary"),
            vmem_limit_bytes=VMEM_LIMIT),
    )(ctab, aqt, ak, avt, lam_vecs, g)


def _attn_b_kernel(c_ref, sink_ref, qt_ref, k_ref, vt_ref, o_ref, s_ref, p_ref, *, bq, seq):
    i = pl.program_id(1)
    win = bq + 2 * WINDOW
    ks = pl.multiple_of(jnp.clip(i * bq - WINDOW, 0, seq - win), WINDOW)
    c0 = ks // WINDOW
    qts = [qt_ref[t * LANES:(t + 1) * LANES, :] for t in range(B_HEADS)]

    mx = [None] * B_HEADS
    for r in range(win // SUB_KEYS):
        r0 = r * SUB_KEYS
        dist = jnp.abs(lax.broadcasted_iota(jnp.int32, (SUB_KEYS, bq), 0)
                       - lax.broadcasted_iota(jnp.int32, (SUB_KEYS, bq), 1)
                       + (ks + r0 - i * bq))
        in_band = dist <= WINDOW
        dist = dist.astype(F32)
        for t in range(B_HEADS):
            g = t // (B_HEADS // B_KV)
            s = jnp.dot(k_ref[pl.ds(ks + r0, SUB_KEYS), g * LANES:(g + 1) * LANES], qts[t],
                        preferred_element_type=F32)
            s = jnp.where(in_band, s - c_ref[t, :, :1] * dist, NEG_BIG)
            s_ref[t, r0:r0 + SUB_KEYS, :] = s
            pm = _fold8(s, jnp.maximum)
            mx[t] = pm if mx[t] is None else jnp.maximum(mx[t], pm)

    outs = []
    for t in range(B_HEADS):
        g = t // (B_HEADS // B_KV)
        sink = sink_ref[t, :, :1]
        m = jnp.maximum(jnp.max(mx[t], axis=0, keepdims=True), sink)
        for r in range(win // SUB_KEYS):
            r0 = r * SUB_KEYS
            p_ref[t, r0:r0 + SUB_KEYS, :] = jnp.exp2(s_ref[t, r0:r0 + SUB_KEYS, :] - m).astype(BF16)
        vt = jnp.concatenate([vt_ref[c0 + u, g * C_VT_ROWS:(g + 1) * C_VT_ROWS, :]
                              for u in range(win // WINDOW)], axis=1)
        acc = jnp.dot(vt, p_ref[t], preferred_element_type=F32)
        denom = acc[HEAD_DIM:HEAD_DIM + 1, :] + jnp.exp2(sink - m)
        outs.append(acc[0:HEAD_DIM, :] / denom)
    o_ref[...] = jnp.concatenate(outs, axis=0).T.astype(BF16)


def _attn_b(ctab, sinks, bqt, bk, bvt, *, batch, seq, bq):
    n = bk.shape[0]
    nq = seq // bq
    win = bq + 2 * WINDOW
    tab = lambda b, i: (0, 0, 0)
    return pl.pallas_call(
        functools.partial(_attn_b_kernel, bq=bq, seq=seq),
        grid=(batch, nq),
        in_specs=[
            pl.BlockSpec((B_HEADS, 1, LANES), tab),
            pl.BlockSpec((B_HEADS, 1, LANES), tab),
            pl.BlockSpec((None, B_HEADS * LANES, bq), lambda b, i: (b, 0, i)),
            pl.BlockSpec((seq, B_KV * LANES), lambda b, i: (b, 0)),
            pl.BlockSpec((None, seq // WINDOW, B_KV * C_VT_ROWS, WINDOW), lambda b, i: (b, 0, 0, 0)),
        ],
        out_specs=pl.BlockSpec((bq, B_HEADS * HEAD_DIM), lambda b, i: (b * nq + i, 0)),
        out_shape=jax.ShapeDtypeStruct((n, B_HEADS * HEAD_DIM), BF16),
        name="attn_window",
        scratch_shapes=[pltpu.VMEM((B_HEADS, win, bq), F32),
                        pltpu.VMEM((B_HEADS, win, bq), BF16)],
        compiler_params=pltpu.CompilerParams(
            dimension_semantics=("arbitrary", "arbitrary"),
            vmem_limit_bytes=VMEM_LIMIT),
    )(ctab, sinks, bqt, bk, bvt)


def _attn_c_kernel(qt_ref, k_ref, vt_ref, o_ref, acc_ref, s_ref, p_ref, *, bq, bk, seq):
    nk = seq // bk
    nstream = acc_ref.shape[0]
    qts = [qt_ref[t * LANES:(t + 1) * LANES, :] for t in range(nstream)]
    acc_ref[...] = jnp.zeros(acc_ref.shape, F32)

    def issue_scores(j):
        maxes = []
        for t in range(nstream):
            mx = None
            for r in range(bk // SUB_KEYS):
                r0 = j * bk + r * SUB_KEYS
                s = jnp.dot(k_ref[r0:r0 + SUB_KEYS, (t // 2) * LANES:(t // 2 + 1) * LANES], qts[t],
                            preferred_element_type=F32)
                s_ref[j % nslot, t, r * SUB_KEYS:(r + 1) * SUB_KEYS, :] = s
                pm = _fold8(s, jnp.maximum)
                mx = pm if mx is None else jnp.maximum(mx, pm)
            maxes.append(jnp.max(mx, axis=0, keepdims=True))
        return maxes

    nslot = s_ref.shape[0]
    ahead = nslot - 1
    m = [jnp.full((1, bq), NEG_BIG, F32)] * nstream
    pending = [issue_scores(j) for j in range(min(ahead, nk))]
    for j in range(nk):
        if j + ahead < nk:
            pending.append(issue_scores(j + ahead))
        maxes = pending.pop(0)
        for t in range(nstream):
            m_new = jnp.maximum(m[t], maxes[t])
            alpha = jnp.exp2(m[t] - m_new)
            m[t] = m_new
            for r in range(bk // SUB_KEYS):
                r0 = r * SUB_KEYS
                p = jnp.exp2(s_ref[j % nslot, t, r0:r0 + SUB_KEYS, :] - m_new)
                p_ref[j % 2, t, r0:r0 + SUB_KEYS, :] = p.astype(BF16)
            acc_ref[t] = alpha * acc_ref[t] + jnp.dot(
                vt_ref[j, (t // 2) * C_VT_ROWS:(t // 2 + 1) * C_VT_ROWS, :], p_ref[j % 2, t],
                preferred_element_type=F32)
    ot = jnp.concatenate([acc_ref[t, 0:HEAD_DIM, :] / acc_ref[t, HEAD_DIM:HEAD_DIM + 1, :]
                          for t in range(nstream)], axis=0)
    o_ref[...] = ot.T.astype(BF16)


def _attn_c(cqt, ck, cvt, *, batch, seq, bq, bk):
    n = ck.shape[0]
    nq = seq // bq
    return pl.pallas_call(
        functools.partial(_attn_c_kernel, bq=bq, bk=bk, seq=seq),
        grid=(batch, nq),
        in_specs=[
            pl.BlockSpec((None, C_HEADS * LANES, bq), lambda b, i: (b, 0, i)),
            pl.BlockSpec((seq, C_KV * LANES), lambda b, i: (b, 0)),
            pl.BlockSpec((None, seq // bk, C_KV * C_VT_ROWS, bk), lambda b, i: (b, 0, 0, 0)),
        ],
        out_specs=pl.BlockSpec((bq, C_HEADS * HEAD_DIM), lambda b, i: (b * nq + i, 0)),
        out_shape=jax.ShapeDtypeStruct((n, C_HEADS * HEAD_DIM), BF16),
        name="attn_global",
        scratch_shapes=[pltpu.VMEM((C_HEADS, C_VT_ROWS, bq), F32),
                        pltpu.VMEM((C_SCORE_SLOTS, C_HEADS, bk, bq), F32),
                        pltpu.VMEM((2, C_HEADS, bk, bq), BF16)],
        compiler_params=pltpu.CompilerParams(
            dimension_semantics=("arbitrary", "arbitrary"),
            vmem_limit_bytes=VMEM_LIMIT),
    )(cqt, ck, cvt)


def _out_mlp_kernel(x_ref, a_ref, b_ref, c_ref, wo_ref, go_ref, g1_ref, w1_ref, w2_ref, g2_ref,
                    o_ref, *, fc):
    mix = jnp.dot(a_ref[...], wo_ref[0:512, :], preferred_element_type=F32)
    mix += jnp.dot(b_ref[...], wo_ref[512:768, :], preferred_element_type=F32)
    mix += jnp.dot(c_ref[...], wo_ref[768:1024, :], preferred_element_type=F32)
    x = x_ref[...] + _rms(mix, go_ref[...])
    h = _rms(x, g1_ref[...]).astype(BF16)
    y = jnp.zeros(x.shape, F32)
    for f in range(D_FF // fc):
        t = jnp.dot(h, w1_ref[:, f * fc:(f + 1) * fc], preferred_element_type=F32)
        t = jnp.square(jnp.maximum(t, 0.0)).astype(BF16)
        y += jnp.dot(t, w2_ref[f * fc:(f + 1) * fc, :], preferred_element_type=F32)
    o_ref[...] = x + _rms(y, g2_ref[...])


def _out_mlp(x2, a, b, c, wo, go, g1, w1, w2, g2, *, bm, fc):
    n = x2.shape[0]
    row = lambda i: (i, 0)
    fixed = lambda i: (0, 0)
    once = dict(pipeline_mode=pl.Buffered(1))
    return pl.pallas_call(
        functools.partial(_out_mlp_kernel, fc=fc),
        grid=(n // bm,),
        in_specs=[
            pl.BlockSpec((bm, D_MODEL), row),
            pl.BlockSpec((bm, 512), row),
            pl.BlockSpec((bm, 256), row),
            pl.BlockSpec((bm, 256), row),
            pl.BlockSpec((D_MODEL, D_MODEL), fixed, **once),
            pl.BlockSpec((1, D_MODEL), fixed),
            pl.BlockSpec((1, D_MODEL), fixed),
            pl.BlockSpec((D_MODEL, D_FF), fixed, **once),
            pl.BlockSpec((D_FF, D_MODEL), fixed, **once),
            pl.BlockSpec((1, D_MODEL), fixed),
        ],
        out_specs=pl.BlockSpec((bm, D_MODEL), row),
        out_shape=jax.ShapeDtypeStruct((n, D_MODEL), F32),
        name="out_mlp",
        compiler_params=pltpu.CompilerParams(
            dimension_semantics=("arbitrary",), vmem_limit_bytes=VMEM_LIMIT),
    )(x2, a, b, c, wo, go, g1, w1, w2, g2)


def _rope_tables(seq):
    pos = jnp.arange(seq)
    row = (pos // GRID_W).astype(F32)
    col = (pos % GRID_W).astype(F32)
    half = HEAD_DIM // 2
    freqs = ROPE_THETA ** (-jnp.arange(0, half, 2, dtype=F32) / half)
    d = jnp.arange(HEAD_DIM)
    ang = jnp.where((d < half)[None, :], row[:, None], col[:, None]) * freqs[d % 16][None, :]
    first = ((d % half) < 16)[None, :]
    cos, sin = jnp.cos(ang), jnp.sin(ang)
    return cos, jnp.where(first, -sin, 0.0), jnp.where(first, 0.0, sin)


def _alibi_tables(seq):
    pos = jnp.arange(seq)
    parts = [((pos // 64) * 64).astype(F32), (pos % 64).astype(F32)]
    pos4 = [jnp.broadcast_to(p[None, :], (A_HEADS, seq)) for p in parts * 2]
    c = jnp.asarray(A_SLOPES, F32) * LOG2E
    c_hi = c.astype(BF16).astype(F32)
    c_lo = (c - c_hi).astype(BF16).astype(F32)
    c4 = [jnp.broadcast_to(v[:, None], (A_HEADS, seq)) for v in (c_hi, c_hi, c_lo, c_lo)]
    kaug = jnp.zeros((A_HEADS, seq, LANES), F32).at[:, :, HEAD_DIM:HEAD_DIM + 8].set(
        jnp.stack([-v for v in c4] + pos4, axis=-1))
    qaugt = jnp.zeros((A_HEADS, HEAD_DIM, seq), F32).at[:, 0:8, :].set(
        jnp.stack(pos4 + c4, axis=1))
    c_tab = jnp.broadcast_to(c[:, None, None], (A_HEADS, 1, LANES))
    return kaug, qaugt, c_tab


def kernel(x, w_in, w_out, g_pre_mix, g_post_mix, lam_q1, lam_k1, lam_q2, lam_k2,
           diff_subln_g, sink_logits, c_q_norm, c_k_norm, g_pre_mlp, g_post_mlp,
           w_mlp_in, w_mlp_out):
    batch, seq, _ = x.shape
    depth = w_in.shape[0]
    n = batch * seq
    x2 = x.reshape(n, D_MODEL)
    tabs64 = _rope_tables(seq)
    tabs = tuple(jnp.tile(t, (1, 2)) for t in tabs64)
    tabs_t = tuple(t.T for t in tabs64)
    lane = jnp.arange(LANES)
    grp = jnp.where((lane[:, None] // HEAD_DIM) == (lane[None, :] // HEAD_DIM),
                    1.0 / HEAD_DIM, 0.0).astype(F32)
    kaug, qaugt, a_ctab = _alibi_tables(seq)
    b_ctab = jnp.broadcast_to((jnp.asarray(B_SLOPES, F32) * LOG2E)[:, None, None],
                              (B_HEADS, 1, LANES))
    row = lambda v: v.reshape(1, -1).astype(F32)
    bk = KEY_CHUNK

    for l in range(depth):
        lam_init = 0.8 - 0.6 * math.exp(-0.3 * l)
        w = w_in[l].astype(BF16)
        wr = jnp.concatenate([w[:, AK0:AV0], w[:, BK0:BV0], w[:, CK0:CV0]], axis=1)
        wt = jnp.concatenate([w[:, AQ0:AK0], w[:, AV0:BQ0], w[:, CQ0:CK0], w[:, CV0:],
                              w[:, BQ0:BK0], w[:, BV0:CQ0]], axis=1).T
        gqt = jnp.broadcast_to(c_q_norm[l].astype(F32)[:, None], (HEAD_DIM, LANES))
        aqt, ak, avt, bqt, bkk, bvt, cqt, ck, cvt = _proj(
            x2, row(g_pre_mix[l]), wr, wt, tabs, tabs_t, row(jnp.tile(c_k_norm[l], 2)), gqt, grp,
            kaug, qaugt, batch=batch, seq=seq, bm=ROW_BLOCK, bk=bk)
        lam_vecs = jnp.stack([lam_q1[l], lam_k1[l], lam_q2[l], lam_k2[l]]).astype(F32)
        a_out = _attn_a(a_ctab, aqt, ak, avt, lam_vecs, row(diff_subln_g[l]), lam_init=lam_init,
                        batch=batch, seq=seq, bq=A_QUERY_BLOCK, bk=bk,
                        hps=A_HEADS_PER_STEP, nslot=A_SCORE_SLOTS)
        sinks = jnp.broadcast_to((sink_logits[l].astype(F32) * LOG2E)[:, None, None],
                                 (B_HEADS, 1, LANES))
        b_out = _attn_b(b_ctab, sinks, bqt, bkk, bvt, batch=batch, seq=seq, bq=B_QUERY_BLOCK)
        c_out = _attn_c(cqt, ck, cvt, batch=batch, seq=seq, bq=C_QUERY_BLOCK, bk=bk)
        x2 = _out_mlp(x2, a_out, b_out, c_out, w_out[l].astype(BF16), row(g_post_mix[l]),
                      row(g_pre_mlp[l]), w_mlp_in[l].astype(BF16), w_mlp_out[l].astype(BF16),
                      row(g_post_mlp[l]), bm=ROW_BLOCK, fc=FF_CHUNK)
    return x2.reshape(batch, seq, D_MODEL)
```

```python
import functools
import math

import jax
import jax.numpy as jnp
from jax import lax
from jax.experimental import pallas as pl
from jax.experimental.pallas import tpu as pltpu

D_MODEL = 1024
HEAD_DIM = 64
LANES = 128
A_HEADS = 4
B_HEADS = 4
B_KV = 2
C_HEADS = 4
C_KV = 2
WINDOW = 128
GRID_W = 64
ROPE_THETA = 10000.0
D_FF = 4 * D_MODEL
EPS = 1e-6
N_ALIBI = A_HEADS + B_HEADS
ALIBI_SLOPES = tuple(2.0 ** (-8.0 * (i + 1) / N_ALIBI) for i in range(N_ALIBI))
A_SLOPES = ALIBI_SLOPES[0::2]
B_SLOPES = ALIBI_SLOPES[1::2]
SCALE = HEAD_DIM ** -0.5
LOG2E = math.log2(math.e)
SCALE_LOG2E = SCALE * LOG2E
NEG_BIG = -1e30

AQ0, AK0, AV0 = 0, 512, 1024
BQ0, BK0, BV0 = 1536, 1792, 1920
CQ0, CK0, CV0 = 2048, 2304, 2432

VMEM_LIMIT = 56 * 1024 * 1024

ROW_BLOCK = 1024
FF_CHUNK = 512
KEY_CHUNK = 512
A_QUERY_BLOCK = 256
C_QUERY_BLOCK = 256
B_QUERY_BLOCK = 256
SUB_KEYS = 128
SUM_ROWS = 16
A_VT_ROWS = LANES + SUM_ROWS
C_VT_ROWS = HEAD_DIM + SUM_ROWS
A_HEADS_PER_STEP = 2
A_SCORE_SLOTS = 3
C_SCORE_SLOTS = 3

F32 = jnp.float32
BF16 = jnp.bfloat16
NT_DIMS = (((1,), (1,)), ((), ()))
TN_DIMS = (((0,), (0,)), ((), ()))


def _rms(x, g):
    ms = jnp.mean(x * x, axis=-1, keepdims=True)
    return x * lax.rsqrt(ms + EPS) * g


def _proj_kernel(x_ref, g_ref, wr_ref, wt_ref, cos_ref, s1_ref, s2_ref, gk_ref, grp_ref,
                 cost_ref, s1t_ref, s2t_ref, gqt_ref, kaug_ref, qaugt_ref,
                 aqt_ref, ak_ref, avt_ref, bqt_ref, bk_ref, bvt_ref, cqt_ref, ck_ref, cvt_ref):
    bm = x_ref.shape[0]
    h =_rms(x_ref[...], g_ref[...]).astype(BF16)
    low = lax.broadcasted_iota(jnp.int32, (bm, LANES), 1) < HEAD_DIM

    def seg(c0, width):
        full = jnp.dot(h, wr_ref[:, c0:c0 + width], preferred_element_type=F32)
        return [full[:, j * LANES:(j + 1) * LANES] for j in range(width // LANES)]

    def seg_t(r0, rows):
        return lax.dot_general(wt_ref[r0:r0 + rows, :], h, NT_DIMS, preferred_element_type=F32)

    def put_slots(ref, j, blk, pad=0.0):
        ref[:, (2 * j) * LANES:(2 * j + 1) * LANES] = jnp.where(low, blk, pad).astype(BF16)
        ref[:, (2 * j + 1) * LANES:(2 * j + 2) * LANES] = jnp.where(
            low, pltpu.roll(blk, HEAD_DIM, 1), pad).astype(BF16)

    def put_slots_t(ref, xt, pads=None):
        zeros = jnp.zeros((HEAD_DIM, bm), BF16)
        for s in range(xt.shape[0] // HEAD_DIM):
            ref[s * LANES:s * LANES + HEAD_DIM, :] = xt[s * HEAD_DIM:(s + 1) * HEAD_DIM].astype(BF16)
            ref[s * LANES + HEAD_DIM:(s + 1) * LANES, :] = zeros if pads is None else pads[s]

    def put_chunks_t(ref, xt, width):
        chunk = ref.shape[-1]
        ones_row = jnp.where(lax.broadcasted_iota(jnp.int32, (SUM_ROWS, chunk), 0) == 0,
                             1.0, 0.0).astype(BF16)
        ext = width + SUM_ROWS
        for c in range(bm // chunk):
            for hh in range(xt.shape[0] // width):
                ref[c, hh * ext:hh * ext + width, :] = xt[hh * width:(hh + 1) * width,
                                                          c * chunk:(c + 1) * chunk].astype(BF16)
                ref[c, hh * ext + width:(hh + 1) * ext, :] = ones_row

    def norm_rope(blk, gain):
        ms = jnp.dot(blk * blk, grp_ref[...], preferred_element_type=F32,
                     precision=lax.Precision.HIGHEST)
        y = blk * lax.rsqrt(ms + EPS) * gain
        return (y * cos_ref[...] + pltpu.roll(y, LANES - 16, 1) * s1_ref[...]
                + pltpu.roll(y, 16, 1) * s2_ref[...])

    def norm_rope_t(xt):
        ms = jnp.mean(xt * xt, axis=0, keepdims=True)
        y = xt * lax.rsqrt(ms + EPS) * jnp.tile(gqt_ref[...], (1, bm // LANES))
        up = jnp.concatenate([y[16:], y[:16]], axis=0)
        dn = jnp.concatenate([y[HEAD_DIM - 16:], y[:HEAD_DIM - 16]], axis=0)
        return y * cost_ref[...] + up * s1t_ref[...] + dn * s2t_ref[...]

    for j, blk in enumerate(seg(0, 512)):
        put_slots(ak_ref, j, blk, kaug_ref[j])
    kb, kc = seg(512, 256)
    put_slots(bk_ref, 0, kb)
    put_slots(ck_ref, 0, norm_rope(kc, gk_ref[...]))

    put_slots_t(bqt_ref, seg_t(1408, 256) * SCALE_LOG2E)
    put_chunks_t(bvt_ref, seg_t(1664, 128), HEAD_DIM)
    put_slots_t(aqt_ref, seg_t(0, 512) * SCALE_LOG2E,
                [qaugt_ref[s // 2].astype(BF16) for s in range(2 * A_HEADS)])
    put_chunks_t(avt_ref, seg_t(512, 512), LANES)
    cqt = seg_t(1024, 256)
    cqt = jnp.concatenate([norm_rope_t(cqt[hh * HEAD_DIM:(hh + 1) * HEAD_DIM])
                           for hh in range(C_HEADS)], axis=0)
    put_slots_t(cqt_ref, cqt * SCALE_LOG2E)
    put_chunks_t(cvt_ref, seg_t(1280, 128), HEAD_DIM)


def _proj(x2, g, wr, wt, tabs, tabs_t, gk, gqt, grp, kaug, qaugt, *, batch, seq, bm, bk):
    n = x2.shape[0]
    nsb = seq // bm
    cpb = bm // bk
    row = lambda i: (i, 0)
    fixed = lambda i: (0, 0)
    tab = lambda i: (i % nsb, 0)
    tab_t = lambda i: (0, i % nsb)
    tok_t = lambda i: (i // nsb, 0, i % nsb)
    chunk_t = lambda i: (i // nsb, i % nsb, 0, 0)
    out_specs = [
        pl.BlockSpec((None, 8 * LANES, bm), tok_t),
        pl.BlockSpec((bm, 8 * LANES), row),
        pl.BlockSpec((None, cpb, A_HEADS * A_VT_ROWS, bk), chunk_t),
        pl.BlockSpec((None, 4 * LANES, bm), tok_t),
        pl.BlockSpec((bm, 2 * LANES), row),
        pl.BlockSpec((None, bm // WINDOW, B_KV * C_VT_ROWS, WINDOW), chunk_t),
        pl.BlockSpec((None, 4 * LANES, bm), tok_t),
        pl.BlockSpec((bm, 2 * LANES), row),
        pl.BlockSpec((None, cpb, C_KV * C_VT_ROWS, bk), chunk_t),
    ]
    out_shape = [
        jax.ShapeDtypeStruct((batch, 8 * LANES, seq), BF16),
        jax.ShapeDtypeStruct((n, 8 * LANES), BF16),
        jax.ShapeDtypeStruct((batch, seq // bk, A_HEADS * A_VT_ROWS, bk), BF16),
        jax.ShapeDtypeStruct((batch, 4 * LANES, seq), BF16),
        jax.ShapeDtypeStruct((n, 2 * LANES), BF16),
        jax.ShapeDtypeStruct((batch, seq // WINDOW, B_KV * C_VT_ROWS, WINDOW), BF16),
        jax.ShapeDtypeStruct((batch, 4 * LANES, seq), BF16),
        jax.ShapeDtypeStruct((n, 2 * LANES), BF16),
        jax.ShapeDtypeStruct((batch, seq // bk, C_KV * C_VT_ROWS, bk), BF16),
    ]
    return pl.pallas_call(
        _proj_kernel,
        grid=(n // bm,),
        in_specs=[
            pl.BlockSpec((bm, D_MODEL), row),
            pl.BlockSpec((1, D_MODEL), fixed),
            pl.BlockSpec(wr.shape, fixed, pipeline_mode=pl.Buffered(1)),
            pl.BlockSpec(wt.shape, fixed, pipeline_mode=pl.Buffered(1)),
            pl.BlockSpec((bm, LANES), tab),
            pl.BlockSpec((bm, LANES), tab),
            pl.BlockSpec((bm, LANES), tab),
            pl.BlockSpec((1, LANES), fixed),
            pl.BlockSpec((LANES, LANES), fixed),
            pl.BlockSpec((HEAD_DIM, bm), tab_t),
            pl.BlockSpec((HEAD_DIM, bm), tab_t),
            pl.BlockSpec((HEAD_DIM, bm), tab_t),
            pl.BlockSpec((HEAD_DIM, LANES), fixed),
            pl.BlockSpec((A_HEADS, bm, LANES), lambda i: (0, i % nsb, 0)),
            pl.BlockSpec((A_HEADS, HEAD_DIM, bm), lambda i: (0, 0, i % nsb)),
        ],
        out_specs=out_specs,
        out_shape=out_shape,
        name="in_proj",
        compiler_params=pltpu.CompilerParams(
            dimension_semantics=("arbitrary",), vmem_limit_bytes=VMEM_LIMIT),
    )(x2, g, wr, wt, *tabs, gk, grp, *tabs_t, gqt, kaug, qaugt)


def _fold8(x, op):
    out = x[0:8]
    for r in range(8, x.shape[0], 8):
        out = op(out, x[r:r + 8])
    return out


def _attn_a_kernel(c_ref, qt_ref, k_ref, vt_ref, lam_ref, g_ref,
                   o_ref, acc_ref, s_ref, p_ref, *, lam_init, bq, bk, seq):
    i = pl.program_id(2)
    nk = seq // bk
    nstream = acc_ref.shape[0]
    ndiag = max(1, bq // bk)
    jd = (i * bq) // bk
    aug = lax.broadcasted_iota(jnp.int32, (LANES, bq), 0) >= HEAD_DIM
    qts = [qt_ref[t * LANES:(t + 1) * LANES, :] for t in range(nstream)]
    acc_ref[...] = jnp.zeros(acc_ref.shape, F32)

    def issue_scores(jj):
        j = jd + jj
        j = jnp.where(j >= nk, j - nk, j)
        k0 = pl.multiple_of(j * bk, bk)
        maxes = []
        for t in range(nstream):
            if jj < ndiag:
                qv = jnp.where(aug, jnp.zeros_like(qts[t]), qts[t])
            else:
                qv = jnp.where(jnp.logical_and(aug, j > jd), -qts[t], qts[t])
            mx = None
            for r in range(bk // SUB_KEYS):
                r0 = r * SUB_KEYS
                s = jnp.dot(k_ref[pl.ds(k0 + r0, SUB_KEYS), t * LANES:(t + 1) * LANES], qv,
                            preferred_element_type=F32)
                if jj < ndiag:
                    rowcol = (lax.broadcasted_iota(jnp.int32, (SUB_KEYS, bq), 0)
                              - lax.broadcasted_iota(jnp.int32, (SUB_KEYS, bq), 1)
                              + (j * bk + r0 - i * bq)).astype(F32)
                    s = s - c_ref[t // 2, :, :1] * jnp.abs(rowcol)
                s_ref[jj % nslot, t, r0:r0 + SUB_KEYS, :] = s
                pm = _fold8(s, jnp.maximum)
                mx = pm if mx is None else jnp.maximum(mx, pm)
            maxes.append(jnp.max(mx, axis=0, keepdims=True))
        return j, maxes

    nslot = s_ref.shape[0]
    ahead = nslot - 1
    m = [jnp.full((1, bq), NEG_BIG, F32)] * nstream
    pending = [issue_scores(jj) for jj in range(min(ahead, nk))]
    for jj in range(nk):
        if jj + ahead < nk:
            pending.append(issue_scores(jj + ahead))
        j, maxes = pending.pop(0)
        for t in range(nstream):
            m_new = jnp.maximum(m[t], maxes[t])
            alpha = jnp.exp2(m[t] - m_new)
            m[t] = m_new
            for r in range(bk // SUB_KEYS):
                r0 = r * SUB_KEYS
                p = jnp.exp2(s_ref[jj % nslot, t, r0:r0 + SUB_KEYS, :] - m_new)
                p_ref[jj % 2, t, r0:r0 + SUB_KEYS, :] = p.astype(BF16)
            acc_ref[t] = alpha * acc_ref[t] + jnp.dot(
                vt_ref[j, (t // 2) * A_VT_ROWS:(t // 2 + 1) * A_VT_ROWS, :], p_ref[jj % 2, t],
                preferred_element_type=F32)

    lam = (jnp.exp(jnp.sum(lam_ref[0:1] * lam_ref[1:2], axis=-1, keepdims=True))
           - jnp.exp(jnp.sum(lam_ref[2:3] * lam_ref[3:4], axis=-1, keepdims=True))
           + lam_init)

    def normalised(t):
        return acc_ref[t, 0:LANES, :] / acc_ref[t, LANES:LANES + 1, :]

    gt = jnp.tile(g_ref[...], (1, bq // LANES))
    for hh in range(nstream // 2):
        ot = normalised(2 * hh) - lam * normalised(2 * hh + 1)
        ms = jnp.mean(ot * ot, axis=0, keepdims=True)
        o_ref[hh * LANES:(hh + 1) * LANES, :] = (
            ot * lax.rsqrt(ms + EPS) * gt * (1.0 - lam_init)).astype(BF16)


def _attn_a(ctab, aqt, ak, avt, lam_vecs, g, *, lam_init, batch, seq, bq, bk, hps, nslot):
    n = ak.shape[0]
    nq = seq // bq
    vec = lambda b, h, i: (0, 0)
    kern = functools.partial(_attn_a_kernel, lam_init=lam_init, bq=bq, bk=bk, seq=seq)
    return pl.pallas_call(
        kern,
        grid=(batch, A_HEADS // hps, nq),
        in_specs=[
            pl.BlockSpec((hps, 1, LANES), lambda b, h, i: (h, 0, 0)),
            pl.BlockSpec((None, hps * 2 * LANES, bq), lambda b, h, i: (b, h, i)),
            pl.BlockSpec((seq, hps * 2 * LANES), lambda b, h, i: (b, h)),
            pl.BlockSpec((None, seq // bk, hps * A_VT_ROWS, bk), lambda b, h, i: (b, 0, h, 0)),
            pl.BlockSpec((4, HEAD_DIM), vec),
            pl.BlockSpec((LANES, LANES), vec),
        ],
        out_specs=pl.BlockSpec((None, hps * LANES, bq), lambda b, h, i: (b, h, i)),
        out_shape=jax.ShapeDtypeStruct((batch, A_HEADS * LANES, seq), BF16),
        name="attn_diff",
        scratch_shapes=[pltpu.VMEM((2 * hps, A_VT_ROWS, bq), F32),
                        pltpu.VMEM((nslot, 2 * hps, bk, bq), F32),
                        pltpu.VMEM((2, 2 * hps, bk, bq), BF16)],
        compiler_params=pltpu.CompilerParams(
            dimension_semantics=("arbitrary", "arbitrary", "arbitrary"),
            vmem_limit_bytes=VMEM_LIMIT),
    )(ctab, aqt, ak, avt, lam_vecs, g)


def _attn_b_kernel(c_ref, sink_ref, qt_ref, k_ref, vt_ref, o_ref, s_ref, p_ref, *, bq, seq):
    i = pl.program_id(1)
    win = bq + 2 * WINDOW
    ks = pl.multiple_of(jnp.clip(i * bq - WINDOW, 0, seq - win), WINDOW)
    c0 = ks // WINDOW
    qts = [qt_ref[t * LANES:(t + 1) * LANES, :] for t in range(B_HEADS)]

    mx = [None] * B_HEADS
    for r in range(win // SUB_KEYS):
        r0 = r * SUB_KEYS
        dist = jnp.abs(lax.broadcasted_iota(jnp.int32, (SUB_KEYS, bq), 0)
                       - lax.broadcasted_iota(jnp.int32, (SUB_KEYS, bq), 1)
                       + (ks + r0 - i * bq))
        in_band = dist <= WINDOW
        dist = dist.astype(F32)
        for t in range(B_HEADS):
            g = t // (B_HEADS // B_KV)
            s = jnp.dot(k_ref[pl.ds(ks + r0, SUB_KEYS), g * LANES:(g + 1) * LANES], qts[t],
                        preferred_element_type=F32)
            s = jnp.where(in_band, s - c_ref[t, :, :1] * dist, NEG_BIG)
            s_ref[t, r0:r0 + SUB_KEYS, :] = s
            pm = _fold8(s, jnp.maximum)
            mx[t] = pm if mx[t] is None else jnp.maximum(mx[t], pm)

    outs = []
    for t in range(B_HEADS):
        g = t // (B_HEADS // B_KV)
        sink = sink_ref[t, :, :1]
        m = jnp.maximum(jnp.max(mx[t], axis=0, keepdims=True), sink)
        for r in range(win // SUB_KEYS):
            r0 = r * SUB_KEYS
            p_ref[t, r0:r0 + SUB_KEYS, :] = jnp.exp2(s_ref[t, r0:r0 + SUB_KEYS, :] - m).astype(BF16)
        vt = jnp.concatenate([vt_ref[c0 + u, g * C_VT_ROWS:(g + 1) * C_VT_ROWS, :]
                              for u in range(win // WINDOW)], axis=1)
        acc = jnp.dot(vt, p_ref[t], preferred_element_type=F32)
        denom = acc[HEAD_DIM:HEAD_DIM + 1, :] + jnp.exp2(sink - m)
        outs.append(acc[0:HEAD_DIM, :] / denom)
    o_ref[...] = jnp.concatenate(outs, axis=0).astype(BF16)


def _attn_b(ctab, sinks, bqt, bk, bvt, *, batch, seq, bq):
    n = bk.shape[0]
    nq = seq // bq
    win = bq + 2 * WINDOW
    tab = lambda b, i: (0, 0, 0)
    return pl.pallas_call(
        functools.partial(_attn_b_kernel, bq=bq, seq=seq),
        grid=(batch, nq),
        in_specs=[
            pl.BlockSpec((B_HEADS, 1, LANES), tab),
            pl.BlockSpec((B_HEADS, 1, LANES), tab),
            pl.BlockSpec((None, B_HEADS * LANES, bq), lambda b, i: (b, 0, i)),
            pl.BlockSpec((seq, B_KV * LANES), lambda b, i: (b, 0)),
            pl.BlockSpec((None, seq // WINDOW, B_KV * C_VT_ROWS, WINDOW), lambda b, i: (b, 0, 0, 0)),
        ],
        out_specs=pl.BlockSpec((None, B_HEADS * HEAD_DIM, bq), lambda b, i: (b, 0, i)),
        out_shape=jax.ShapeDtypeStruct((batch, B_HEADS * HEAD_DIM, seq), BF16),
        name="attn_window",
        scratch_shapes=[pltpu.VMEM((B_HEADS, win, bq), F32),
                        pltpu.VMEM((B_HEADS, win, bq), BF16)],
        compiler_params=pltpu.CompilerParams(
            dimension_semantics=("arbitrary", "arbitrary"),
            vmem_limit_bytes=VMEM_LIMIT),
    )(ctab, sinks, bqt, bk, bvt)


def _attn_c_kernel(qt_ref, k_ref, vt_ref, o_ref, acc_ref, s_ref, p_ref, *, bq, bk, seq):
    nk = seq // bk
    nstream = acc_ref.shape[0]
    qts = [qt_ref[t * LANES:(t + 1) * LANES, :] for t in range(nstream)]
    acc_ref[...] = jnp.zeros(acc_ref.shape, F32)

    def issue_scores(j):
        maxes = []
        for t in range(nstream):
            mx = None
            for r in range(bk // SUB_KEYS):
                r0 = j * bk + r * SUB_KEYS
                s = jnp.dot(k_ref[r0:r0 + SUB_KEYS, (t // 2) * LANES:(t // 2 + 1) * LANES], qts[t],
                            preferred_element_type=F32)
                s_ref[j % nslot, t, r * SUB_KEYS:(r + 1) * SUB_KEYS, :] = s
                pm = _fold8(s, jnp.maximum)
                mx = pm if mx is None else jnp.maximum(mx, pm)
            maxes.append(jnp.max(mx, axis=0, keepdims=True))
        return maxes

    nslot = s_ref.shape[0]
    ahead = nslot - 1
    m = [jnp.full((1, bq), NEG_BIG, F32)] * nstream
    pending = [issue_scores(j) for j in range(min(ahead, nk))]
    for j in range(nk):
        if j + ahead < nk:
            pending.append(issue_scores(j + ahead))
        maxes = pending.pop(0)
        for t in range(nstream):
            m_new = jnp.maximum(m[t], maxes[t])
            alpha = jnp.exp2(m[t] - m_new)
            m[t] = m_new
            for r in range(bk // SUB_KEYS):
                r0 = r * SUB_KEYS
                p = jnp.exp2(s_ref[j % nslot, t, r0:r0 + SUB_KEYS, :] - m_new)
                p_ref[j % 2, t, r0:r0 + SUB_KEYS, :] = p.astype(BF16)
            acc_ref[t] = alpha * acc_ref[t] + jnp.dot(
                vt_ref[j, (t // 2) * C_VT_ROWS:(t // 2 + 1) * C_VT_ROWS, :], p_ref[j % 2, t],
                preferred_element_type=F32)
    ot = jnp.concatenate([acc_ref[t, 0:HEAD_DIM, :] / acc_ref[t, HEAD_DIM:HEAD_DIM + 1, :]
                          for t in range(nstream)], axis=0)
    o_ref[...] = ot.astype(BF16)


def _attn_c(cqt, ck, cvt, *, batch, seq, bq, bk):
    n = ck.shape[0]
    nq = seq // bq
    return pl.pallas_call(
        functools.partial(_attn_c_kernel, bq=bq, bk=bk, seq=seq),
        grid=(batch, nq),
        in_specs=[
            pl.BlockSpec((None, C_HEADS * LANES, bq), lambda b, i: (b, 0, i)),
            pl.BlockSpec((seq, C_KV * LANES), lambda b, i: (b, 0)),
            pl.BlockSpec((None, seq // bk, C_KV * C_VT_ROWS, bk), lambda b, i: (b, 0, 0, 0)),
        ],
        out_specs=pl.BlockSpec((None, C_HEADS * HEAD_DIM, bq), lambda b, i: (b, 0, i)),
        out_shape=jax.ShapeDtypeStruct((batch, C_HEADS * HEAD_DIM, seq), BF16),
        name="attn_global",
        scratch_shapes=[pltpu.VMEM((C_HEADS, C_VT_ROWS, bq), F32),
                        pltpu.VMEM((C_SCORE_SLOTS, C_HEADS, bk, bq), F32),
                        pltpu.VMEM((2, C_HEADS, bk, bq), BF16)],
        compiler_params=pltpu.CompilerParams(
            dimension_semantics=("arbitrary", "arbitrary"),
            vmem_limit_bytes=VMEM_LIMIT),
    )(cqt, ck, cvt)


def _out_mlp_kernel(x_ref, a_ref, b_ref, c_ref, wo_ref, go_ref, g1_ref, w1_ref, w2_ref, g2_ref,
                    o_ref, *, fc):
    mix = lax.dot_general(a_ref[...], wo_ref[0:512, :], TN_DIMS, preferred_element_type=F32)
    mix += lax.dot_general(b_ref[...], wo_ref[512:768, :], TN_DIMS, preferred_element_type=F32)
    mix += lax.dot_general(c_ref[...], wo_ref[768:1024, :], TN_DIMS, preferred_element_type=F32)
    x = x_ref[...] + _rms(mix, go_ref[...])
    h = _rms(x, g1_ref[...]).astype(BF16)
    y = jnp.zeros(x.shape, F32)
    for f in range(D_FF // fc):
        t = jnp.dot(h, w1_ref[:, f * fc:(f + 1) * fc], preferred_element_type=F32)
        t = jnp.square(jnp.maximum(t, 0.0)).astype(BF16)
        y += jnp.dot(t, w2_ref[f * fc:(f + 1) * fc, :], preferred_element_type=F32)
    o_ref[...] = x + _rms(y, g2_ref[...])


def _out_mlp(x2, a, b, c, wo, go, g1, w1, w2, g2, *, seq, bm, fc):
    n = x2.shape[0]
    nsb = seq // bm
    tok_t = lambda i: (i // nsb, 0, i % nsb)
    row = lambda i: (i, 0)
    fixed = lambda i: (0, 0)
    once = dict(pipeline_mode=pl.Buffered(1))
    return pl.pallas_call(
        functools.partial(_out_mlp_kernel, fc=fc),
        grid=(n // bm,),
        in_specs=[
            pl.BlockSpec((bm, D_MODEL), row),
            pl.BlockSpec((None, 512, bm), tok_t),
            pl.BlockSpec((None, 256, bm), tok_t),
            pl.BlockSpec((None, 256, bm), tok_t),
            pl.BlockSpec((D_MODEL, D_MODEL), fixed, **once),
            pl.BlockSpec((1, D_MODEL), fixed),
            pl.BlockSpec((1, D_MODEL), fixed),
            pl.BlockSpec((D_MODEL, D_FF), fixed, **once),
            pl.BlockSpec((D_FF, D_MODEL), fixed, **once),
            pl.BlockSpec((1, D_MODEL), fixed),
        ],
        out_specs=pl.BlockSpec((bm, D_MODEL), row),
        out_shape=jax.ShapeDtypeStruct((n, D_MODEL), F32),
        name="out_mlp",
        compiler_params=pltpu.CompilerParams(
            dimension_semantics=("arbitrary",), vmem_limit_bytes=VMEM_LIMIT),
    )(x2, a, b, c, wo, go, g1, w1, w2, g2)


def _rope_tables(seq):
    pos = jnp.arange(seq)
    row = (pos // GRID_W).astype(F32)
    col = (pos % GRID_W).astype(F32)
    half = HEAD_DIM // 2
    freqs = ROPE_THETA ** (-jnp.arange(0, half, 2, dtype=F32) / half)
    d = jnp.arange(HEAD_DIM)
    ang = jnp.where((d < half)[None, :], row[:, None], col[:, None]) * freqs[d % 16][None, :]
    first = ((d % half) < 16)[None, :]
    cos, sin = jnp.cos(ang), jnp.sin(ang)
    return cos, jnp.where(first, -sin, 0.0), jnp.where(first, 0.0, sin)


def _alibi_tables(seq):
    pos = jnp.arange(seq)
    parts = [((pos // 64) * 64).astype(F32), (pos % 64).astype(F32)]
    pos4 = [jnp.broadcast_to(p[None, :], (A_HEADS, seq)) for p in parts * 2]
    c = jnp.asarray(A_SLOPES, F32) * LOG2E
    c_hi = c.astype(BF16).astype(F32)
    c_lo = (c - c_hi).astype(BF16).astype(F32)
    c4 = [jnp.broadcast_to(v[:, None], (A_HEADS, seq)) for v in (c_hi, c_hi, c_lo, c_lo)]
    kaug = jnp.zeros((A_HEADS, seq, LANES), F32).at[:, :, HEAD_DIM:HEAD_DIM + 8].set(
        jnp.stack([-v for v in c4] + pos4, axis=-1))
    qaugt = jnp.zeros((A_HEADS, HEAD_DIM, seq), F32).at[:, 0:8, :].set(
        jnp.stack(pos4 + c4, axis=1))
    c_tab = jnp.broadcast_to(c[:, None, None], (A_HEADS, 1, LANES))
    return kaug, qaugt, c_tab


def kernel(x, w_in, w_out, g_pre_mix, g_post_mix, lam_q1, lam_k1, lam_q2, lam_k2,
           diff_subln_g, sink_logits, c_q_norm, c_k_norm, g_pre_mlp, g_post_mlp,
           w_mlp_in, w_mlp_out):
    batch, seq, _ = x.shape
    depth = w_in.shape[0]
    n = batch * seq
    x2 = x.reshape(n, D_MODEL)
    tabs64 = _rope_tables(seq)
    tabs = tuple(jnp.tile(t, (1, 2)) for t in tabs64)
    tabs_t = tuple(t.T for t in tabs64)
    lane = jnp.arange(LANES)
    grp = jnp.where((lane[:, None] // HEAD_DIM) == (lane[None, :] // HEAD_DIM),
                    1.0 / HEAD_DIM, 0.0).astype(F32)
    kaug, qaugt, a_ctab = _alibi_tables(seq)
    b_ctab = jnp.broadcast_to((jnp.asarray(B_SLOPES, F32) * LOG2E)[:, None, None],
                              (B_HEADS, 1, LANES))
    row = lambda v: v.reshape(1, -1).astype(F32)
    bk = KEY_CHUNK

    for l in range(depth):
        lam_init = 0.8 - 0.6 * math.exp(-0.3 * l)
        w = w_in[l].astype(BF16)
        wr = jnp.concatenate([w[:, AK0:AV0], w[:, BK0:BV0], w[:, CK0:CV0]], axis=1)
        wt = jnp.concatenate([w[:, AQ0:AK0], w[:, AV0:BQ0], w[:, CQ0:CK0], w[:, CV0:],
                              w[:, BQ0:BK0], w[:, BV0:CQ0]], axis=1).T
        gqt = jnp.broadcast_to(c_q_norm[l].astype(F32)[:, None], (HEAD_DIM, LANES))
        aqt, ak, avt, bqt, bkk, bvt, cqt, ck, cvt = _proj(
            x2, row(g_pre_mix[l]), wr, wt, tabs, tabs_t, row(jnp.tile(c_k_norm[l], 2)), gqt, grp,
            kaug, qaugt, batch=batch, seq=seq, bm=ROW_BLOCK, bk=bk)
        lam_vecs = jnp.stack([lam_q1[l], lam_k1[l], lam_q2[l], lam_k2[l]]).astype(F32)
        subln_gt = jnp.broadcast_to(diff_subln_g[l].astype(F32)[:, None], (LANES, LANES))
        a_out = _attn_a(a_ctab, aqt, ak, avt, lam_vecs, subln_gt, lam_init=lam_init,
                        batch=batch, seq=seq, bq=A_QUERY_BLOCK, bk=bk,
                        hps=A_HEADS_PER_STEP, nslot=A_SCORE_SLOTS)
        sinks = jnp.broadcast_to((sink_logits[l].astype(F32) * LOG2E)[:, None, None],
                                 (B_HEADS, 1, LANES))
        b_out = _attn_b(b_ctab, sinks, bqt, bkk, bvt, batch=batch, seq=seq, bq=B_QUERY_BLOCK)
        c_out = _attn_c(cqt, ck, cvt, batch=batch, seq=seq, bq=C_QUERY_BLOCK, bk=bk)
        x2 = _out_mlp(x2, a_out, b_out, c_out, w_out[l].astype(BF16), row(g_post_mix[l]),
                      row(g_pre_mlp[l]), w_mlp_in[l].astype(BF16), w_mlp_out[l].astype(BF16),
                      row(g_post_mlp[l]), seq=seq, bm=ROW_BLOCK, fc=FF_CHUNK)
    return x2.reshape(batch, seq, D_MODEL)
```

```python
import functools
import math

import jax
import jax.numpy as jnp
from jax import lax
from jax.experimental import pallas as pl
from jax.experimental.pallas import tpu as pltpu

D_MODEL = 1024
HEAD_DIM = 64
LANES = 128
A_HEADS = 4
B_HEADS = 4
B_KV = 2
C_HEADS = 4
C_KV = 2
WINDOW = 128
GRID_W = 64
ROPE_THETA = 10000.0
D_FF = 4 * D_MODEL
EPS = 1e-6
N_ALIBI = A_HEADS + B_HEADS
ALIBI_SLOPES = tuple(2.0 ** (-8.0 * (i + 1) / N_ALIBI) for i in range(N_ALIBI))
A_SLOPES = ALIBI_SLOPES[0::2]
B_SLOPES = ALIBI_SLOPES[1::2]
SCALE = HEAD_DIM ** -0.5
LOG2E = math.log2(math.e)
SCALE_LOG2E = SCALE * LOG2E
NEG_BIG = -1e30

AQ0, AK0, AV0 = 0, 512, 1024
BQ0, BK0, BV0 = 1536, 1792, 1920
CQ0, CK0, CV0 = 2048, 2304, 2432

VMEM_LIMIT = 56 * 1024 * 1024

ROW_BLOCK = 1024
FF_CHUNK = 512
A_KEY_CHUNK = 512
C_KEY_CHUNK = 256
A_QUERY_BLOCK = 256
C_QUERY_BLOCK = 256
B_QUERY_BLOCK = 256
SUB_KEYS = 128
SUM_ROWS = 16
A_VT_ROWS = LANES + SUM_ROWS
C_VT_ROWS = HEAD_DIM + SUM_ROWS
A_HEADS_PER_STEP = 2
A_SCORE_SLOTS = 3
C_SCORE_SLOTS = 3

F32 = jnp.float32
BF16 = jnp.bfloat16
NT_DIMS = (((1,), (1,)), ((), ()))
TN_DIMS = (((0,), (0,)), ((), ()))


def _rms(x, g):
    ms = jnp.mean(x * x, axis=-1, keepdims=True)
    return x * lax.rsqrt(ms + EPS) * g


def _proj_kernel(x_ref, g_ref, wr_ref, wt_ref, cos_ref, s1_ref, s2_ref, gk_ref, grp_ref,
                 cost_ref, s1t_ref, s2t_ref, gqt_ref, kaug_ref, qaugt_ref,
                 aqt_ref, ak_ref, avt_ref, bqt_ref, bk_ref, bvt_ref, cqt_ref, ck_ref, cvt_ref):
    bm = x_ref.shape[0]
    h =_rms(x_ref[...], g_ref[...]).astype(BF16)
    low = lax.broadcasted_iota(jnp.int32, (bm, LANES), 1) < HEAD_DIM

    def seg(c0, width):
        full = jnp.dot(h, wr_ref[:, c0:c0 + width], preferred_element_type=F32)
        return [full[:, j * LANES:(j + 1) * LANES] for j in range(width // LANES)]

    def seg_t(r0, rows):
        return lax.dot_general(wt_ref[r0:r0 + rows, :], h, NT_DIMS, preferred_element_type=F32)

    def put_slots(ref, j, blk, pad=0.0):
        ref[:, (2 * j) * LANES:(2 * j + 1) * LANES] = jnp.where(low, blk, pad).astype(BF16)
        ref[:, (2 * j + 1) * LANES:(2 * j + 2) * LANES] = jnp.where(
            low, pltpu.roll(blk, HEAD_DIM, 1), pad).astype(BF16)

    def put_slots_t(ref, xt, pads=None):
        zeros = jnp.zeros((HEAD_DIM, bm), BF16)
        for s in range(xt.shape[0] // HEAD_DIM):
            ref[s * LANES:s * LANES + HEAD_DIM, :] = xt[s * HEAD_DIM:(s + 1) * HEAD_DIM].astype(BF16)
            ref[s * LANES + HEAD_DIM:(s + 1) * LANES, :] = zeros if pads is None else pads[s]

    def put_chunks_t(ref, xt, width):
        chunk = ref.shape[-1]
        ones_row = jnp.where(lax.broadcasted_iota(jnp.int32, (SUM_ROWS, chunk), 0) == 0,
                             1.0, 0.0).astype(BF16)
        ext = width + SUM_ROWS
        for c in range(bm // chunk):
            for hh in range(xt.shape[0] // width):
                ref[c, hh * ext:hh * ext + width, :] = xt[hh * width:(hh + 1) * width,
                                                          c * chunk:(c + 1) * chunk].astype(BF16)
                ref[c, hh * ext + width:(hh + 1) * ext, :] = ones_row

    def norm_rope(blk, gain):
        ms = jnp.dot(blk * blk, grp_ref[...], preferred_element_type=F32,
                     precision=lax.Precision.HIGHEST)
        y = blk * lax.rsqrt(ms + EPS) * gain
        return (y * cos_ref[...] + pltpu.roll(y, LANES - 16, 1) * s1_ref[...]
                + pltpu.roll(y, 16, 1) * s2_ref[...])

    def norm_rope_t(xt):
        ms = jnp.mean(xt * xt, axis=0, keepdims=True)
        y = xt * lax.rsqrt(ms + EPS) * jnp.tile(gqt_ref[...], (1, bm // LANES))
        up = jnp.concatenate([y[16:], y[:16]], axis=0)
        dn = jnp.concatenate([y[HEAD_DIM - 16:], y[:HEAD_DIM - 16]], axis=0)
        return y * cost_ref[...] + up * s1t_ref[...] + dn * s2t_ref[...]

    for j, blk in enumerate(seg(0, 512)):
        put_slots(ak_ref, j, blk, kaug_ref[j])
    kb, kc = seg(512, 256)
    put_slots(bk_ref, 0, kb)
    put_slots(ck_ref, 0, norm_rope(kc, gk_ref[...]))

    put_slots_t(bqt_ref, seg_t(1408, 256) * SCALE_LOG2E)
    put_chunks_t(bvt_ref, seg_t(1664, 128), HEAD_DIM)
    put_slots_t(aqt_ref, seg_t(0, 512) * SCALE_LOG2E,
                [qaugt_ref[s // 2].astype(BF16) for s in range(2 * A_HEADS)])
    put_chunks_t(avt_ref, seg_t(512, 512), LANES)
    cqt = seg_t(1024, 256)
    cqt = jnp.concatenate([norm_rope_t(cqt[hh * HEAD_DIM:(hh + 1) * HEAD_DIM])
                           for hh in range(C_HEADS)], axis=0)
    put_slots_t(cqt_ref, cqt * SCALE_LOG2E)
    put_chunks_t(cvt_ref, seg_t(1280, 128), HEAD_DIM)


def _proj(x2, g, wr, wt, tabs, tabs_t, gk, gqt, grp, kaug, qaugt, *, batch, seq, bm, bk_a, bk_c):
    n = x2.shape[0]
    nsb = seq // bm
    row = lambda i: (i, 0)
    fixed = lambda i: (0, 0)
    tab = lambda i: (i % nsb, 0)
    tab_t = lambda i: (0, i % nsb)
    tok_t = lambda i: (i // nsb, 0, i % nsb)
    chunk_t = lambda i: (i // nsb, i % nsb, 0, 0)
    out_specs = [
        pl.BlockSpec((None, 8 * LANES, bm), tok_t),
        pl.BlockSpec((bm, 8 * LANES), row),
        pl.BlockSpec((None, bm // bk_a, A_HEADS * A_VT_ROWS, bk_a), chunk_t),
        pl.BlockSpec((None, 4 * LANES, bm), tok_t),
        pl.BlockSpec((bm, 2 * LANES), row),
        pl.BlockSpec((None, bm // WINDOW, B_KV * C_VT_ROWS, WINDOW), chunk_t),
        pl.BlockSpec((None, 4 * LANES, bm), tok_t),
        pl.BlockSpec((bm, 2 * LANES), row),
        pl.BlockSpec((None, bm // bk_c, C_KV * C_VT_ROWS, bk_c), chunk_t),
    ]
    out_shape = [
        jax.ShapeDtypeStruct((batch, 8 * LANES, seq), BF16),
        jax.ShapeDtypeStruct((n, 8 * LANES), BF16),
        jax.ShapeDtypeStruct((batch, seq // bk_a, A_HEADS * A_VT_ROWS, bk_a), BF16),
        jax.ShapeDtypeStruct((batch, 4 * LANES, seq), BF16),
        jax.ShapeDtypeStruct((n, 2 * LANES), BF16),
        jax.ShapeDtypeStruct((batch, seq // WINDOW, B_KV * C_VT_ROWS, WINDOW), BF16),
        jax.ShapeDtypeStruct((batch, 4 * LANES, seq), BF16),
        jax.ShapeDtypeStruct((n, 2 * LANES), BF16),
        jax.ShapeDtypeStruct((batch, seq // bk_c, C_KV * C_VT_ROWS, bk_c), BF16),
    ]
    return pl.pallas_call(
        _proj_kernel,
        grid=(n // bm,),
        in_specs=[
            pl.BlockSpec((bm, D_MODEL), row),
            pl.BlockSpec((1, D_MODEL), fixed),
            pl.BlockSpec(wr.shape, fixed, pipeline_mode=pl.Buffered(1)),
            pl.BlockSpec(wt.shape, fixed, pipeline_mode=pl.Buffered(1)),
            pl.BlockSpec((bm, LANES), tab),
            pl.BlockSpec((bm, LANES), tab),
            pl.BlockSpec((bm, LANES), tab),
            pl.BlockSpec((1, LANES), fixed),
            pl.BlockSpec((LANES, LANES), fixed),
            pl.BlockSpec((HEAD_DIM, bm), tab_t),
            pl.BlockSpec((HEAD_DIM, bm), tab_t),
            pl.BlockSpec((HEAD_DIM, bm), tab_t),
            pl.BlockSpec((HEAD_DIM, LANES), fixed),
            pl.BlockSpec((A_HEADS, bm, LANES), lambda i: (0, i % nsb, 0)),
            pl.BlockSpec((A_HEADS, HEAD_DIM, bm), lambda i: (0, 0, i % nsb)),
        ],
        out_specs=out_specs,
        out_shape=out_shape,
        name="in_proj",
        compiler_params=pltpu.CompilerParams(
            dimension_semantics=("arbitrary",), vmem_limit_bytes=VMEM_LIMIT),
    )(x2, g, wr, wt, *tabs, gk, grp, *tabs_t, gqt, kaug, qaugt)


def _fold8(x, op):
    out = x[0:8]
    for r in range(8, x.shape[0], 8):
        out = op(out, x[r:r + 8])
    return out


def _attn_a_kernel(c_ref, qt_ref, k_ref, vt_ref, lam_ref, g_ref,
                   o_ref, acc_ref, s_ref, p_ref, *, lam_init, bq, bk, seq):
    i = pl.program_id(2)
    nk = seq // bk
    nstream = acc_ref.shape[0]
    ndiag = max(1, bq // bk)
    jd = (i * bq) // bk
    aug = lax.broadcasted_iota(jnp.int32, (LANES, bq), 0) >= HEAD_DIM
    qts = [qt_ref[t * LANES:(t + 1) * LANES, :] for t in range(nstream)]
    acc_ref[...] = jnp.zeros(acc_ref.shape, F32)

    def chunk_index(jj):
        j = jd + jj
        return jnp.where(j >= nk, j - nk, j)

    def issue_scores(jj, t):
        j = chunk_index(jj)
        k0 = pl.multiple_of(j * bk, bk)
        if jj < ndiag:
            qv = jnp.where(aug, jnp.zeros_like(qts[t]), qts[t])
        else:
            qv = jnp.where(jnp.logical_and(aug, j > jd), -qts[t], qts[t])
        mx = None
        for r in range(bk // SUB_KEYS):
            r0 = r * SUB_KEYS
            s = jnp.dot(k_ref[pl.ds(k0 + r0, SUB_KEYS), t * LANES:(t + 1) * LANES], qv,
                        preferred_element_type=F32)
            if jj < ndiag:
                rowcol = (lax.broadcasted_iota(jnp.int32, (SUB_KEYS, bq), 0)
                          - lax.broadcasted_iota(jnp.int32, (SUB_KEYS, bq), 1)
                          + (j * bk + r0 - i * bq)).astype(F32)
                s = s - c_ref[t // 2, :, :1] * jnp.abs(rowcol)
            s_ref[jj % nslot, t, r0:r0 + SUB_KEYS, :] = s
            pm = _fold8(s, jnp.maximum)
            mx = pm if mx is None else jnp.maximum(mx, pm)
        return jnp.max(mx, axis=0, keepdims=True)

    def softmax_pv(jj, t, mx):
        j = chunk_index(jj)
        m_new = jnp.maximum(m[t], mx)
        alpha = jnp.exp2(m[t] - m_new)
        m[t] = m_new
        for r in range(bk // SUB_KEYS):
            r0 = r * SUB_KEYS
            p = jnp.exp2(s_ref[jj % nslot, t, r0:r0 + SUB_KEYS, :] - m_new)
            p_ref[jj % 2, t, r0:r0 + SUB_KEYS, :] = p.astype(BF16)
        acc_ref[t] = alpha * acc_ref[t] + jnp.dot(
            vt_ref[j, (t // 2) * A_VT_ROWS:(t // 2 + 1) * A_VT_ROWS, :], p_ref[jj % 2, t],
            preferred_element_type=F32)

    nslot = s_ref.shape[0]
    ahead = nslot - 1
    m = [jnp.full((1, bq), NEG_BIG, F32)] * nstream
    pending = [[issue_scores(jj, t) for jj in range(min(ahead, nk))] for t in range(nstream)]
    for jj in range(nk):
        for t in range(nstream):
            softmax_pv(jj, t, pending[t].pop(0))
            if jj + ahead < nk:
                pending[t].append(issue_scores(jj + ahead, t))

    lam = (jnp.exp(jnp.sum(lam_ref[0:1] * lam_ref[1:2], axis=-1, keepdims=True))
           - jnp.exp(jnp.sum(lam_ref[2:3] * lam_ref[3:4], axis=-1, keepdims=True))
           + lam_init)

    def normalised(t):
        return acc_ref[t, 0:LANES, :] / acc_ref[t, LANES:LANES + 1, :]

    gt = jnp.tile(g_ref[...], (1, bq // LANES))
    for hh in range(nstream // 2):
        ot = normalised(2 * hh) - lam * normalised(2 * hh + 1)
        ms = jnp.mean(ot * ot, axis=0, keepdims=True)
        o_ref[hh * LANES:(hh + 1) * LANES, :] = (
            ot * lax.rsqrt(ms + EPS) * gt * (1.0 - lam_init)).astype(BF16)


def _attn_a(ctab, aqt, ak, avt, lam_vecs, g, *, lam_init, batch, seq, bq, bk, hps, nslot):
    n = ak.shape[0]
    nq = seq // bq
    vec = lambda b, h, i: (0, 0)
    kern = functools.partial(_attn_a_kernel, lam_init=lam_init, bq=bq, bk=bk, seq=seq)
    return pl.pallas_call(
        kern,
        grid=(batch, A_HEADS // hps, nq),
        in_specs=[
            pl.BlockSpec((hps, 1, LANES), lambda b, h, i: (h, 0, 0)),
            pl.BlockSpec((None, hps * 2 * LANES, bq), lambda b, h, i: (b, h, i)),
            pl.BlockSpec((seq, hps * 2 * LANES), lambda b, h, i: (b, h)),
            pl.BlockSpec((None, seq // bk, hps * A_VT_ROWS, bk), lambda b, h, i: (b, 0, h, 0)),
            pl.BlockSpec((4, HEAD_DIM), vec),
            pl.BlockSpec((LANES, LANES), vec),
        ],
        out_specs=pl.BlockSpec((None, hps * LANES, bq), lambda b, h, i: (b, h, i)),
        out_shape=jax.ShapeDtypeStruct((batch, A_HEADS * LANES, seq), BF16),
        name="attn_diff",
        scratch_shapes=[pltpu.VMEM((2 * hps, A_VT_ROWS, bq), F32),
                        pltpu.VMEM((nslot, 2 * hps, bk, bq), F32),
                        pltpu.VMEM((2, 2 * hps, bk, bq), BF16)],
        compiler_params=pltpu.CompilerParams(
            dimension_semantics=("arbitrary", "arbitrary", "arbitrary"),
            vmem_limit_bytes=VMEM_LIMIT),
    )(ctab, aqt, ak, avt, lam_vecs, g)


def _attn_b_kernel(c_ref, sink_ref, qt_ref, k_ref, vt_ref, o_ref, s_ref, p_ref, *, bq, seq):
    i = pl.program_id(1)
    win = bq + 2 * WINDOW
    ks = pl.multiple_of(jnp.clip(i * bq - WINDOW, 0, seq - win), WINDOW)
    c0 = ks // WINDOW
    qts = [qt_ref[t * LANES:(t + 1) * LANES, :] for t in range(B_HEADS)]

    mx = [None] * B_HEADS
    for r in range(win // SUB_KEYS):
        r0 = r * SUB_KEYS
        dist = jnp.abs(lax.broadcasted_iota(jnp.int32, (SUB_KEYS, bq), 0)
                       - lax.broadcasted_iota(jnp.int32, (SUB_KEYS, bq), 1)
                       + (ks + r0 - i * bq))
        in_band = dist <= WINDOW
        dist = dist.astype(F32)
        for t in range(B_HEADS):
            g = t // (B_HEADS // B_KV)
            s = jnp.dot(k_ref[pl.ds(ks + r0, SUB_KEYS), g * LANES:(g + 1) * LANES], qts[t],
                        preferred_element_type=F32)
            s = jnp.where(in_band, s - c_ref[t, :, :1] * dist, NEG_BIG)
            s_ref[t, r0:r0 + SUB_KEYS, :] = s
            pm = _fold8(s, jnp.maximum)
            mx[t] = pm if mx[t] is None else jnp.maximum(mx[t], pm)

    outs = []
    for t in range(B_HEADS):
        g = t // (B_HEADS // B_KV)
        sink = sink_ref[t, :, :1]
        m = jnp.maximum(jnp.max(mx[t], axis=0, keepdims=True), sink)
        for r in range(win // SUB_KEYS):
            r0 = r * SUB_KEYS
            p_ref[t, r0:r0 + SUB_KEYS, :] = jnp.exp2(s_ref[t, r0:r0 + SUB_KEYS, :] - m).astype(BF16)
        vt = jnp.concatenate([vt_ref[c0 + u, g * C_VT_ROWS:(g + 1) * C_VT_ROWS, :]
                              for u in range(win // WINDOW)], axis=1)
        acc = jnp.dot(vt, p_ref[t], preferred_element_type=F32)
        denom = acc[HEAD_DIM:HEAD_DIM + 1, :] + jnp.exp2(sink - m)
        outs.append(acc[0:HEAD_DIM, :] / denom)
    o_ref[...] = jnp.concatenate(outs, axis=0).astype(BF16)


def _attn_b(ctab, sinks, bqt, bk, bvt, *, batch, seq, bq):
    n = bk.shape[0]
    nq = seq // bq
    win = bq + 2 * WINDOW
    tab = lambda b, i: (0, 0, 0)
    return pl.pallas_call(
        functools.partial(_attn_b_kernel, bq=bq, seq=seq),
        grid=(batch, nq),
        in_specs=[
            pl.BlockSpec((B_HEADS, 1, LANES), tab),
            pl.BlockSpec((B_HEADS, 1, LANES), tab),
            pl.BlockSpec((None, B_HEADS * LANES, bq), lambda b, i: (b, 0, i)),
            pl.BlockSpec((seq, B_KV * LANES), lambda b, i: (b, 0)),
            pl.BlockSpec((None, seq // WINDOW, B_KV * C_VT_ROWS, WINDOW), lambda b, i: (b, 0, 0, 0)),
        ],
        out_specs=pl.BlockSpec((None, B_HEADS * HEAD_DIM, bq), lambda b, i: (b, 0, i)),
        out_shape=jax.ShapeDtypeStruct((batch, B_HEADS * HEAD_DIM, seq), BF16),
        name="attn_window",
        scratch_shapes=[pltpu.VMEM((B_HEADS, win, bq), F32),
                        pltpu.VMEM((B_HEADS, win, bq), BF16)],
        compiler_params=pltpu.CompilerParams(
            dimension_semantics=("arbitrary", "arbitrary"),
            vmem_limit_bytes=VMEM_LIMIT),
    )(ctab, sinks, bqt, bk, bvt)


def _attn_c_kernel(qt_ref, k_ref, vt_ref, o_ref, acc_ref, s_ref, p_ref, *, bq, bk, seq):
    nk = seq // bk
    nstream = acc_ref.shape[0]
    qts = [qt_ref[t * LANES:(t + 1) * LANES, :] for t in range(nstream)]
    acc_ref[...] = jnp.zeros(acc_ref.shape, F32)

    def issue_scores(j, t):
        mx = None
        for r in range(bk // SUB_KEYS):
            r0 = j * bk + r * SUB_KEYS
            s = jnp.dot(k_ref[r0:r0 + SUB_KEYS, (t // 2) * LANES:(t // 2 + 1) * LANES], qts[t],
                        preferred_element_type=F32)
            s_ref[j % nslot, t, r * SUB_KEYS:(r + 1) * SUB_KEYS, :] = s
            pm = _fold8(s, jnp.maximum)
            mx = pm if mx is None else jnp.maximum(mx, pm)
        return jnp.max(mx, axis=0, keepdims=True)

    def softmax_pv(j, t, mx):
        m_new = jnp.maximum(m[t], mx)
        alpha = jnp.exp2(m[t] - m_new)
        m[t] = m_new
        for r in range(bk // SUB_KEYS):
            r0 = r * SUB_KEYS
            p = jnp.exp2(s_ref[j % nslot, t, r0:r0 + SUB_KEYS, :] - m_new)
            p_ref[j % 2, t, r0:r0 + SUB_KEYS, :] = p.astype(BF16)
        acc_ref[t] = alpha * acc_ref[t] + jnp.dot(
            vt_ref[j, (t // 2) * C_VT_ROWS:(t // 2 + 1) * C_VT_ROWS, :], p_ref[j % 2, t],
            preferred_element_type=F32)

    nslot = s_ref.shape[0]
    ahead = nslot - 1
    m = [jnp.full((1, bq), NEG_BIG, F32)] * nstream
    pending = [[issue_scores(j, t) for j in range(min(ahead, nk))] for t in range(nstream)]
    for j in range(nk):
        for t in range(nstream):
            softmax_pv(j, t, pending[t].pop(0))
            if j + ahead < nk:
                pending[t].append(issue_scores(j + ahead, t))
    ot = jnp.concatenate([acc_ref[t, 0:HEAD_DIM, :] / acc_ref[t, HEAD_DIM:HEAD_DIM + 1, :]
                          for t in range(nstream)], axis=0)
    o_ref[...] = ot.astype(BF16)


def _attn_c(cqt, ck, cvt, *, batch, seq, bq, bk):
    n = ck.shape[0]
    nq = seq // bq
    return pl.pallas_call(
        functools.partial(_attn_c_kernel, bq=bq, bk=bk, seq=seq),
        grid=(batch, nq),
        in_specs=[
            pl.BlockSpec((None, C_HEADS * LANES, bq), lambda b, i: (b, 0, i)),
            pl.BlockSpec((seq, C_KV * LANES), lambda b, i: (b, 0)),
            pl.BlockSpec((None, seq // bk, C_KV * C_VT_ROWS, bk), lambda b, i: (b, 0, 0, 0)),
        ],
        out_specs=pl.BlockSpec((None, C_HEADS * HEAD_DIM, bq), lambda b, i: (b, 0, i)),
        out_shape=jax.ShapeDtypeStruct((batch, C_HEADS * HEAD_DIM, seq), BF16),
        name="attn_global",
        scratch_shapes=[pltpu.VMEM((C_HEADS, C_VT_ROWS, bq), F32),
                        pltpu.VMEM((C_SCORE_SLOTS, C_HEADS, bk, bq), F32),
                        pltpu.VMEM((2, C_HEADS, bk, bq), BF16)],
        compiler_params=pltpu.CompilerParams(
            dimension_semantics=("arbitrary", "arbitrary"),
            vmem_limit_bytes=VMEM_LIMIT),
    )(cqt, ck, cvt)


def _out_mlp_kernel(x_ref, a_ref, b_ref, c_ref, wo_ref, go_ref, g1_ref, w1_ref, w2_ref, g2_ref,
                    o_ref, *, fc):
    mix = lax.dot_general(a_ref[...], wo_ref[0:512, :], TN_DIMS, preferred_element_type=F32)
    mix += lax.dot_general(b_ref[...], wo_ref[512:768, :], TN_DIMS, preferred_element_type=F32)
    mix += lax.dot_general(c_ref[...], wo_ref[768:1024, :], TN_DIMS, preferred_element_type=F32)
    x = x_ref[...] + _rms(mix, go_ref[...])
    h = _rms(x, g1_ref[...]).astype(BF16)
    y = jnp.zeros(x.shape, F32)
    for f in range(D_FF // fc):
        t = jnp.dot(h, w1_ref[:, f * fc:(f + 1) * fc], preferred_element_type=F32)
        t = jnp.square(jnp.maximum(t, 0.0)).astype(BF16)
        y += jnp.dot(t, w2_ref[f * fc:(f + 1) * fc, :], preferred_element_type=F32)
    o_ref[...] = x + _rms(y, g2_ref[...])


def _out_mlp(x2, a, b, c, wo, go, g1, w1, w2, g2, *, seq, bm, fc):
    n = x2.shape[0]
    nsb = seq // bm
    tok_t = lambda i: (i // nsb, 0, i % nsb)
    row = lambda i: (i, 0)
    fixed = lambda i: (0, 0)
    once = dict(pipeline_mode=pl.Buffered(1))
    return pl.pallas_call(
        functools.partial(_out_mlp_kernel, fc=fc),
        grid=(n // bm,),
        in_specs=[
            pl.BlockSpec((bm, D_MODEL), row),
            pl.BlockSpec((None, 512, bm), tok_t),
            pl.BlockSpec((None, 256, bm), tok_t),
            pl.BlockSpec((None, 256, bm), tok_t),
            pl.BlockSpec((D_MODEL, D_MODEL), fixed, **once),
            pl.BlockSpec((1, D_MODEL), fixed),
            pl.BlockSpec((1, D_MODEL), fixed),
            pl.BlockSpec((D_MODEL, D_FF), fixed, **once),
            pl.BlockSpec((D_FF, D_MODEL), fixed, **once),
            pl.BlockSpec((1, D_MODEL), fixed),
        ],
        out_specs=pl.BlockSpec((bm, D_MODEL), row),
        out_shape=jax.ShapeDtypeStruct((n, D_MODEL), F32),
        name="out_mlp",
        compiler_params=pltpu.CompilerParams(
            dimension_semantics=("arbitrary",), vmem_limit_bytes=VMEM_LIMIT),
    )(x2, a, b, c, wo, go, g1, w1, w2, g2)


def _rope_tables(seq):
    pos = jnp.arange(seq)
    row = (pos // GRID_W).astype(F32)
    col = (pos % GRID_W).astype(F32)
    half = HEAD_DIM // 2
    freqs = ROPE_THETA ** (-jnp.arange(0, half, 2, dtype=F32) / half)
    d = jnp.arange(HEAD_DIM)
    ang = jnp.where((d < half)[None, :], row[:, None], col[:, None]) * freqs[d % 16][None, :]
    first = ((d % half) < 16)[None, :]
    cos, sin = jnp.cos(ang), jnp.sin(ang)
    return cos, jnp.where(first, -sin, 0.0), jnp.where(first, 0.0, sin)


def _alibi_tables(seq):
    pos = jnp.arange(seq)
    parts = [((pos // 64) * 64).astype(F32), (pos % 64).astype(F32)]
    pos4 = [jnp.broadcast_to(p[None, :], (A_HEADS, seq)) for p in parts * 2]
    c = jnp.asarray(A_SLOPES, F32) * LOG2E
    c_hi = c.astype(BF16).astype(F32)
    c_lo = (c - c_hi).astype(BF16).astype(F32)
    c4 = [jnp.broadcast_to(v[:, None], (A_HEADS, seq)) for v in (c_hi, c_hi, c_lo, c_lo)]
    kaug = jnp.zeros((A_HEADS, seq, LANES), F32).at[:, :, HEAD_DIM:HEAD_DIM + 8].set(
        jnp.stack([-v for v in c4] + pos4, axis=-1))
    qaugt = jnp.zeros((A_HEADS, HEAD_DIM, seq), F32).at[:, 0:8, :].set(
        jnp.stack(pos4 + c4, axis=1))
    c_tab = jnp.broadcast_to(c[:, None, None], (A_HEADS, 1, LANES))
    return kaug, qaugt, c_tab


def kernel(x, w_in, w_out, g_pre_mix, g_post_mix, lam_q1, lam_k1, lam_q2, lam_k2,
           diff_subln_g, sink_logits, c_q_norm, c_k_norm, g_pre_mlp, g_post_mlp,
           w_mlp_in, w_mlp_out):
    batch, seq, _ = x.shape
    depth = w_in.shape[0]
    n = batch * seq
    x2 = x.reshape(n, D_MODEL)
    tabs64 = _rope_tables(seq)
    tabs = tuple(jnp.tile(t, (1, 2)) for t in tabs64)
    tabs_t = tuple(t.T for t in tabs64)
    lane = jnp.arange(LANES)
    grp = jnp.where((lane[:, None] // HEAD_DIM) == (lane[None, :] // HEAD_DIM),
                    1.0 / HEAD_DIM, 0.0).astype(F32)
    kaug, qaugt, a_ctab = _alibi_tables(seq)
    b_ctab = jnp.broadcast_to((jnp.asarray(B_SLOPES, F32) * LOG2E)[:, None, None],
                              (B_HEADS, 1, LANES))
    row = lambda v: v.reshape(1, -1).astype(F32)

    for l in range(depth):
        lam_init = 0.8 - 0.6 * math.exp(-0.3 * l)
        w = w_in[l].astype(BF16)
        wr = jnp.concatenate([w[:, AK0:AV0], w[:, BK0:BV0], w[:, CK0:CV0]], axis=1)
        wt = jnp.concatenate([w[:, AQ0:AK0], w[:, AV0:BQ0], w[:, CQ0:CK0], w[:, CV0:],
                              w[:, BQ0:BK0], w[:, BV0:CQ0]], axis=1).T
        gqt = jnp.broadcast_to(c_q_norm[l].astype(F32)[:, None], (HEAD_DIM, LANES))
        aqt, ak, avt, bqt, bkk, bvt, cqt, ck, cvt = _proj(
            x2, row(g_pre_mix[l]), wr, wt, tabs, tabs_t, row(jnp.tile(c_k_norm[l], 2)), gqt, grp,
            kaug, qaugt, batch=batch, seq=seq, bm=ROW_BLOCK, bk_a=A_KEY_CHUNK, bk_c=C_KEY_CHUNK)
        lam_vecs = jnp.stack([lam_q1[l], lam_k1[l], lam_q2[l], lam_k2[l]]).astype(F32)
        subln_gt = jnp.broadcast_to(diff_subln_g[l].astype(F32)[:, None], (LANES, LANES))
        a_out = _attn_a(a_ctab, aqt, ak, avt, lam_vecs, subln_gt, lam_init=lam_init,
                        batch=batch, seq=seq, bq=A_QUERY_BLOCK, bk=A_KEY_CHUNK,
                        hps=A_HEADS_PER_STEP, nslot=A_SCORE_SLOTS)
        sinks = jnp.broadcast_to((sink_logits[l].astype(F32) * LOG2E)[:, None, None],
                                 (B_HEADS, 1, LANES))
        b_out = _attn_b(b_ctab, sinks, bqt, bkk, bvt, batch=batch, seq=seq, bq=B_QUERY_BLOCK)
        c_out = _attn_c(cqt, ck, cvt, batch=batch, seq=seq, bq=C_QUERY_BLOCK, bk=C_KEY_CHUNK)
        x2 = _out_mlp(x2, a_out, b_out, c_out, w_out[l].astype(BF16), row(g_post_mix[l]),
                      row(g_pre_mlp[l]), w_mlp_in[l].astype(BF16), w_mlp_out[l].astype(BF16),
                      row(g_post_mlp[l]), seq=seq, bm=ROW_BLOCK, fc=FF_CHUNK)
    return x2.reshape(batch, seq, D_MODEL)
```
